```python
import jax, jax.numpy as jnp
from jax import lax
import numpy as np

D_MODEL = 1024
BATCH = 8
SEQ = 4096
DEPTH = 1

HEAD_DIM = 64
ROPE_THETA = 10000.0
Q_BLOCK = 128
A_HEADS = 8
A_WIDTH = A_HEADS * HEAD_DIM
A_KV_RANK = 128
A_IDX_HEADS = 4
A_IDX_DIM = 64
A_TOPK_MAX = 256
B_HEADS = 8
B_GROUPS = 2
B_REP = B_HEADS // B_GROUPS
B_WIDTH = B_HEADS * HEAD_DIM
B_KV_WIDTH = B_GROUPS * HEAD_DIM
B_CMP_STRIDE = 16
B_CMP_LEN = 2 * B_CMP_STRIDE
B_CMP_HIDDEN = 256
B_SLC_LEN = 64
B_SLC_TOP = 16
B_WINDOW = 512
D_FF = 2816
MIX_IN_WIDTH = (A_WIDTH + A_KV_RANK + A_IDX_HEADS * A_IDX_DIM + A_IDX_DIM + A_IDX_HEADS
                + B_WIDTH + 6 * B_KV_WIDTH + 3 * B_HEADS + 2 * D_MODEL)
N_MOD = 9
ALPHA = (2.0 * DEPTH) ** 0.25
BETA = (8.0 * DEPTH) ** -0.25
LN_EPS = 1e-5
NEG = -1e30
FORCE_BONUS = 1e4

kernel_name = "hybrid_dsa_nsa_macaron_deepnorm_adaln"


def layer_norm(x, g, b):
    xf = x.astype(jnp.float32)
    mu = jnp.mean(xf, -1, keepdims=True)
    var = jnp.mean(jnp.square(xf - mu), -1, keepdims=True)
    return ((xf - mu) * lax.rsqrt(var + LN_EPS) * g + b).astype(x.dtype)


def rms_norm(x, g):
    xf = x.astype(jnp.float32)
    return (xf * lax.rsqrt(jnp.mean(jnp.square(xf), -1, keepdims=True) + LN_EPS) * g).astype(x.dtype)


def rope(x, pos):
    half = x.shape[-1] // 2
    inv = ROPE_THETA ** (-jnp.arange(half, dtype=jnp.float32) / half)
    ang = pos.astype(jnp.float32)[:, None] * inv[None, :]
    cos = jnp.cos(ang)[:, None, :]
    sin = jnp.sin(ang)[:, None, :]
    xf = x.astype(jnp.float32)
    x1, x2 = xf[..., :half], xf[..., half:]
    return jnp.concatenate([x1 * cos - x2 * sin, x1 * sin + x2 * cos], -1).astype(x.dtype)


def masked_softmax(s, mask):
    s = jnp.where(mask, s.astype(jnp.float32), NEG)
    return jax.nn.softmax(s, axis=-1) * mask


def swiglu(u, w_in, w_out):
    a, b = jnp.split(u @ w_in, 2, axis=-1)
    return (jax.nn.silu(a) * b) @ w_out


def modulate(x, shift, scale):
    return x * (1.0 + scale[:, None, :]) + shift[:, None, :]


def gather_rows(table, idx):
    return jax.vmap(lambda t, i: t[i])(table, idx)


def dsa_mixer(q, kv_lat, qi, ki, wi, kv_norm_g, w_kv_up, pos):
    B, S, _ = q.shape
    q = rope(q.reshape(B, S, A_HEADS, HEAD_DIM), pos)
    k, v = jnp.split(rms_norm(kv_lat, kv_norm_g) @ w_kv_up, 2, axis=-1)
    k = rope(k[:, :, None, :], pos)[:, :, 0]
    qi = rope(qi.reshape(B, S, A_IDX_HEADS, A_IDX_DIM), pos)
    ki = rope(ki[:, :, None, :], pos)[:, :, 0]
    wi = (wi * (A_IDX_HEADS ** -0.5)).astype(jnp.float32)
    topk = min(A_TOPK_MAX, S // 4)
    kpos = jnp.arange(S)

    def block(n):
        q0 = n * Q_BLOCK
        tq = q0 + jnp.arange(Q_BLOCK)
        qb = lax.dynamic_slice_in_dim(q, q0, Q_BLOCK, 1)
        qib = lax.dynamic_slice_in_dim(qi, q0, Q_BLOCK, 1)
        wib = lax.dynamic_slice_in_dim(wi, q0, Q_BLOCK, 1)
        logits = jnp.einsum('bqhd,bsd->bqhs', qib, ki).astype(jnp.float32)
        score = jnp.einsum('bqh,bqhs->bqs', wib, jax.nn.relu(logits))
        score = jnp.where(kpos[None, None, :] <= tq[None, :, None], score, NEG)
        _, sel = lax.top_k(score, topk)
        valid = sel <= tq[None, :, None]
        k_sel = gather_rows(k, sel)
        v_sel = gather_rows(v, sel)
        s = jnp.einsum('bqhd,bqkd->bhqk', qb, k_sel).astype(jnp.float32) * (HEAD_DIM ** -0.5)
        p = masked_softmax(s, valid[:, None])
        return jnp.einsum('bhqk,bqkd->bqhd', p.astype(v.dtype), v_sel)

    o = lax.map(block, jnp.arange(S // Q_BLOCK))
    return jnp.moveaxis(o, 0, 1).reshape(B, S, A_WIDTH)


def compress(t, pe, w1, w2):
    B, S, G, D = t.shape
    chunks = t.reshape(B, S // B_CMP_STRIDE, B_CMP_STRIDE, G, D)
    blocks = jnp.concatenate([chunks[:, :-1], chunks[:, 1:]], axis=2) + pe[:, None, :]
    flat = jnp.transpose(blocks, (0, 1, 3, 2, 4)).reshape(B, -1, G, B_CMP_LEN * D)
    return jax.nn.silu(flat @ w1) @ w2


def nsa_mixer(q, kv6, gates, pe_k, w1_k, w2_k, pe_v, w1_v, w2_v, pos):
    B, S, _ = q.shape
    q = rope(q.reshape(B, S, B_HEADS, HEAD_DIM), pos).reshape(B, S, B_GROUPS, B_REP, HEAD_DIM)
    kc, vc, ks, vs, kw, vw = [t.reshape(B, S, B_GROUPS, HEAD_DIM) for t in jnp.split(kv6, 6, axis=-1)]
    kc, ks, kw = rope(kc, pos), rope(ks, pos), rope(kw, pos)
    k_cmp = compress(kc, pe_k, w1_k, w2_k)
    v_cmp = compress(vc, pe_v, w1_v, w2_v)
    n_cmp = k_cmp.shape[1]
    cmp_start = jnp.arange(n_cmp) * B_CMP_STRIDE
    cmp_end = cmp_start + B_CMP_LEN - 1
    n_slc = S // B_SLC_LEN
    top_n = min(B_SLC_TOP, n_slc)
    ks_blk = jnp.transpose(ks.reshape(B, n_slc, B_SLC_LEN, B_GROUPS, HEAD_DIM), (0, 3, 1, 2, 4))
    vs_blk = jnp.transpose(vs.reshape(B, n_slc, B_SLC_LEN, B_GROUPS, HEAD_DIM), (0, 3, 1, 2, 4))
    slc_start = jnp.arange(n_slc) * B_SLC_LEN
    overlap = ((cmp_start[:, None] < slc_start[None, :] + B_SLC_LEN)
               & (slc_start[None, :] < cmp_start[:, None] + B_CMP_LEN)).astype(jnp.float32)
    jj = jnp.arange(n_slc)
    kw_pad = jnp.pad(kw, ((0, 0), (B_WINDOW, 0), (0, 0), (0, 0)))
    vw_pad = jnp.pad(vw, ((0, 0), (B_WINDOW, 0), (0, 0), (0, 0)))
    scale = HEAD_DIM ** -0.5

    def block(n):
        q0 = n * Q_BLOCK
        tq = q0 + jnp.arange(Q_BLOCK)
        qb = lax.dynamic_slice_in_dim(q, q0, Q_BLOCK, 1)
        s_c = jnp.einsum('bqgrd,bcgd->bgrqc', qb, k_cmp).astype(jnp.float32) * scale
        p_c = masked_softmax(s_c, cmp_end[None, :] <= tq[:, None])
        o_c = jnp.einsum('bgrqc,bcgd->bqgrd', p_c.astype(v_cmp.dtype), v_cmp)
        imp = jnp.einsum('bgrqc,cj->bgqj', p_c, overlap)
        blk_q = tq // B_SLC_LEN
        admissible = jj[None, :] <= blk_q[:, None]
        forced = (jj[None, :] == 0) | (jj[None, :] == blk_q[:, None]) | (jj[None, :] == blk_q[:, None] - 1)
        imp = jnp.where(admissible, imp + FORCE_BONUS * forced.astype(jnp.float32), NEG)
        _, sel = lax.top_k(imp, top_n)
        k_g = jax.vmap(gather_rows)(ks_blk, sel)
        v_g = jax.vmap(gather_rows)(vs_blk, sel).reshape(B, B_GROUPS, Q_BLOCK, top_n * B_SLC_LEN, HEAD_DIM)
        s_s = jnp.einsum('bqgrd,bgqnld->bgrqnl', qb, k_g).astype(jnp.float32) * scale
        s_s = s_s.reshape(B, B_GROUPS, B_REP, Q_BLOCK, top_n * B_SLC_LEN)
        tok = (sel[..., None] * B_SLC_LEN + jnp.arange(B_SLC_LEN)).reshape(B, B_GROUPS, Q_BLOCK, top_n * B_SLC_LEN)
        p_s = masked_softmax(s_s, (tok <= tq[None, None, :, None])[:, :, None])
        o_s = jnp.einsum('bgrqm,bgqmd->bqgrd', p_s.astype(v_g.dtype), v_g)
        kwin = lax.dynamic_slice_in_dim(kw_pad, q0, Q_BLOCK + B_WINDOW, 1)
        vwin = lax.dynamic_slice_in_dim(vw_pad, q0, Q_BLOCK + B_WINDOW, 1)
        kpos = q0 - B_WINDOW + jnp.arange(Q_BLOCK + B_WINDOW)
        mask_w = ((kpos[None, :] <= tq[:, None]) & (kpos[None, :] > tq[:, None] - B_WINDOW)
                  & (kpos[None, :] >= 0))
        s_w = jnp.einsum('bqgrd,bkgd->bgrqk', qb, kwin).astype(jnp.float32) * scale
        p_w = masked_softmax(s_w, mask_w)
        o_w = jnp.einsum('bgrqk,bkgd->bqgrd', p_w.astype(vwin.dtype), vwin)
        return o_c, o_s, o_w

    o_c, o_s, o_w = [jnp.moveaxis(o, 0, 1).reshape(B, S, B_GROUPS, B_REP, HEAD_DIM)
                     for o in lax.map(block, jnp.arange(S // Q_BLOCK))]
    g = jax.nn.sigmoid(gates.astype(jnp.float32)).astype(q.dtype).reshape(B, S, 3, B_GROUPS, B_REP, 1)
    o = g[:, :, 0] * o_c + g[:, :, 1] * o_s + g[:, :, 2] * o_w
    return o.reshape(B, S, B_WIDTH)


def setup_inputs(seed: int = 0) -> dict:
    key = jax.random.key(seed)
    ks = jax.random.split(key, 32)
    L = DEPTH

    def w(k, shape, fan_in, scale=1.0):
        return jax.random.normal(k, (L,) + shape, jnp.float32) * (scale * fan_in ** -0.5)

    def gain(k, n):
        return 1.0 + 0.02 * jax.random.normal(k, (L, n), jnp.float32)

    def bias(k, shape):
        return 0.02 * jax.random.normal(k, (L,) + shape, jnp.float32)

    D = D_MODEL
    return {
        "x": jax.random.normal(ks[0], (BATCH, SEQ, D), jnp.float32),
        "c": jax.random.normal(ks[1], (BATCH, D), jnp.float32),
        "w_ada": w(ks[2], (D, N_MOD * D), D),
        "b_ada": bias(ks[3], (N_MOD * D,)),
        "ffn1_w_in": w(ks[4], (D, 2 * D_FF), D),
        "ffn1_w_out": w(ks[5], (D_FF, D), D_FF, BETA),
        "ln1_g": gain(ks[6], D),
        "ln1_b": bias(ks[7], (D,)),
        "w_mix_in": w(ks[8], (D, MIX_IN_WIDTH), D),
        "kv_norm_g": gain(ks[9], A_KV_RANK),
        "w_kv_up": w(ks[10], (A_KV_RANK, 2 * HEAD_DIM), A_KV_RANK),
        "cmp_pe_k": 0.1 * jax.random.normal(ks[11], (L, B_CMP_LEN, HEAD_DIM), jnp.float32),
        "cmp_w1_k": w(ks[12], (B_CMP_LEN * HEAD_DIM, B_CMP_HIDDEN), B_CMP_LEN * HEAD_DIM),
        "cmp_w2_k": w(ks[13], (B_CMP_HIDDEN, HEAD_DIM), B_CMP_HIDDEN),
        "cmp_pe_v": 0.1 * jax.random.normal(ks[14], (L, B_CMP_LEN, HEAD_DIM), jnp.float32),
        "cmp_w1_v": w(ks[15], (B_CMP_LEN * HEAD_DIM, B_CMP_HIDDEN), B_CMP_LEN * HEAD_DIM),
        "cmp_w2_v": w(ks[16], (B_CMP_HIDDEN, HEAD_DIM), B_CMP_HIDDEN),
        "w_br_a": w(ks[17], (A_WIDTH, D), A_WIDTH),
        "w_br_b": w(ks[18], (B_WIDTH, D), B_WIDTH),
        "w_mix_out": w(ks[19], (D, D), D, BETA),
        "ln2_g": gain(ks[20], D),
        "ln2_b": bias(ks[21], (D,)),
        "ffn2_w_in": w(ks[22], (D, 2 * D_FF), D),
        "ffn2_w_out": w(ks[23], (D_FF, D), D_FF, BETA),
        "ln3_g": gain(ks[24], D),
        "ln3_b": bias(ks[25], (D,)),
    }


def reference(x, c, w_ada, b_ada, ffn1_w_in, ffn1_w_out, ln1_g, ln1_b, w_mix_in, kv_norm_g, w_kv_up,
              cmp_pe_k, cmp_w1_k, cmp_w2_k, cmp_pe_v, cmp_w1_v, cmp_w2_v, w_br_a, w_br_b, w_mix_out,
              ln2_g, ln2_b, ffn2_w_in, ffn2_w_out, ln3_g, ln3_b):
    S = x.shape[1]
    pos = jnp.arange(S)
    split_points = np.cumsum([A_WIDTH, A_KV_RANK, A_IDX_HEADS * A_IDX_DIM, A_IDX_DIM, A_IDX_HEADS,
                              B_WIDTH, 6 * B_KV_WIDTH, 3 * B_HEADS]).tolist()
    c_act = jax.nn.silu(c)
    for l in range(DEPTH):
        mod = c_act @ w_ada[l] + b_ada[l]
        sh1, sc1, g1, sh2, sc2, g2, sh3, sc3, g3 = jnp.split(mod, N_MOD, axis=-1)
        f1 = swiglu(modulate(x, sh1, sc1), ffn1_w_in[l], ffn1_w_out[l])
        x = layer_norm(ALPHA * x + 0.5 * g1[:, None, :] * f1, ln1_g[l], ln1_b[l])
        h = modulate(x, sh2, sc2) @ w_mix_in[l]
        qa, kv_lat, qi, ki, wi, qb, kv6, gb, gm = jnp.split(h, split_points, axis=-1)
        ya = dsa_mixer(qa, kv_lat, qi, ki, wi, kv_norm_g[l], w_kv_up[l], pos)
        yb = nsa_mixer(qb, kv6, gb, cmp_pe_k[l], cmp_w1_k[l], cmp_w2_k[l],
                       cmp_pe_v[l], cmp_w1_v[l], cmp_w2_v[l], pos)
        gate_a, gate_b = jnp.split(jax.nn.sigmoid(gm.astype(jnp.float32)).astype(x.dtype), 2, axis=-1)
        merged = gate_a * (ya @ w_br_a[l]) + gate_b * (yb @ w_br_b[l])
        mix = merged @ w_mix_out[l]
        x = layer_norm(ALPHA * x + g2[:, None, :] * mix, ln2_g[l], ln2_b[l])
        f2 = swiglu(modulate(x, sh3, sc3), ffn2_w_in[l], ffn2_w_out[l])
        x = layer_norm(ALPHA * x + 0.5 * g3[:, None, :] * f2, ln3_g[l], ln3_b[l])
    return x
```

```python
import functools

import numpy as np
import jax
import jax.numpy as jnp
from jax import lax
from jax.experimental import pallas as pl
from jax.experimental.pallas import tpu as pltpu

F32 = jnp.float32
BF16 = jnp.bfloat16
I32 = jnp.int32

HEAD_DIM = 64
ROPE_THETA = 10000.0
A_HEADS = 8
A_WIDTH = A_HEADS * HEAD_DIM
A_KV_RANK = 128
A_IDX_HEADS = 4
A_IDX_DIM = 64
A_TOPK_MAX = 256
B_HEADS = 8
B_GROUPS = 2
B_REP = B_HEADS // B_GROUPS
B_WIDTH = B_HEADS * HEAD_DIM
B_KV_WIDTH = B_GROUPS * HEAD_DIM
B_CMP_STRIDE = 16
B_CMP_LEN = 2 * B_CMP_STRIDE
B_CMP_HIDDEN = 256
B_SLC_LEN = 64
B_SLC_TOP = 16
B_WINDOW = 512
N_MOD = 9
LN_EPS = 1e-5
NEG = -1e30
FORCE_BONUS = 1e4

LANES = 128
Q_BLOCK = 128
KEY_CHUNK = 512
TOKEN_TILE = 512
FF_CHUNK = 256
VMEM_LIMIT = 56 * 1024 * 1024

NEG_KEY = int(np.array(NEG, np.float32).view(np.int32)) ^ 0x7FFFFFFF
INT_MIN = -(2 ** 31)
INT_MAX = 2 ** 31 - 1


def _dot(a, b):
    return jnp.dot(a, b, preferred_element_type=F32)


def _dot_nt(a, b):
    return lax.dot_general(a, b, (((1,), (1,)), ((), ())), preferred_element_type=F32)


def _layer_norm(y, g, b):
    mu = jnp.mean(y, axis=-1, keepdims=True)
    d = y - mu
    var = jnp.mean(d * d, axis=-1, keepdims=True)
    return d * lax.rsqrt(var + LN_EPS) * g + b


def _params(n_axes):
    return pltpu.CompilerParams(dimension_semantics=("arbitrary",) * n_axes,
                                vmem_limit_bytes=VMEM_LIMIT)


def _ada_kernel(c_ref, w_ref, b_ref, o_ref):
    c = c_ref[...]
    ca = c * jax.nn.sigmoid(c)
    o_ref[...] = jnp.dot(ca, w_ref[...], preferred_element_type=F32,
                         precision=lax.Precision.HIGHEST) + b_ref[...]


def _ada(c, w_ada, b_ada):
    bsz, d = c.shape
    n = w_ada.shape[1]
    tn = n // 8
    return pl.pallas_call(
        _ada_kernel,
        grid=(n // tn,),
        in_specs=[pl.BlockSpec((bsz, d), lambda j: (0, 0)),
                  pl.BlockSpec((d, tn), lambda j: (0, j)),
                  pl.BlockSpec((1, tn), lambda j: (0, j))],
        out_specs=pl.BlockSpec((bsz, tn), lambda j: (0, j)),
        out_shape=jax.ShapeDtypeStruct((bsz, n), F32),
        compiler_params=_params(1),
        name="ada",
    )(c, w_ada, b_ada.reshape(1, n))


def _ffn_kernel(x_ref, mod_ref, wa_ref, wb_ref, wo_ref, g_ref, b_ref, o_ref, u_ref, acc_ref,
                *, mod_base, n_chunks, alpha):
    x = x_ref[0]
    sh = mod_ref[0, mod_base:mod_base + 1, :]
    sc = mod_ref[0, mod_base + 1:mod_base + 2, :]
    gt = mod_ref[0, mod_base + 2:mod_base + 3, :]
    u_ref[...] = (x * (1.0 + sc) + sh).astype(BF16)
    acc_ref[...] = jnp.zeros_like(acc_ref)

    def body(j, carry):
        u = u_ref[...]
        a = _dot(u, wa_ref[j])
        b = _dot(u, wb_ref[j])
        h = (a * jax.nn.sigmoid(a) * b).astype(BF16)
        acc_ref[...] += _dot(h, wo_ref[j])
        return carry

    lax.fori_loop(0, n_chunks, body, 0)
    y = alpha * x + (0.5 * gt) * acc_ref[...]
    o_ref[0] = _layer_norm(y, g_ref[...], b_ref[...])


def _ffn(x, mod, w_in, w_out, ln_g, ln_b, mod_base, alpha):
    bsz, s, d = x.shape
    d_ff = w_out.shape[0]
    n_chunks = d_ff // FF_CHUNK
    tm = TOKEN_TILE
    wa = w_in[:, :d_ff].astype(BF16).reshape(d, n_chunks, FF_CHUNK).transpose(1, 0, 2)
    wb = w_in[:, d_ff:].astype(BF16).reshape(d, n_chunks, FF_CHUNK).transpose(1, 0, 2)
    wo = w_out.astype(BF16).reshape(n_chunks, FF_CHUNK, d)
    const3 = lambda b, i: (0, 0, 0)
    return pl.pallas_call(
        functools.partial(_ffn_kernel, mod_base=mod_base, n_chunks=n_chunks, alpha=alpha),
        grid=(bsz, s // tm),
        in_specs=[pl.BlockSpec((1, tm, d), lambda b, i: (b, i, 0)),
                  pl.BlockSpec((1, N_MOD, d), lambda b, i: (b, 0, 0)),
                  pl.BlockSpec((n_chunks, d, FF_CHUNK), const3),
                  pl.BlockSpec((n_chunks, d, FF_CHUNK), const3),
                  pl.BlockSpec((n_chunks, FF_CHUNK, d), const3),
                  pl.BlockSpec((1, d), lambda b, i: (0, 0)),
                  pl.BlockSpec((1, d), lambda b, i: (0, 0))],
        out_specs=pl.BlockSpec((1, tm, d), lambda b, i: (b, i, 0)),
        out_shape=jax.ShapeDtypeStruct((bsz, s, d), F32),
        scratch_shapes=[pltpu.VMEM((tm, d), BF16), pltpu.VMEM((tm, d), F32)],
        compiler_params=_params(2),
        name="ffn",
    )(x, mod, wa, wb, wo, ln_g.reshape(1, d), ln_b.reshape(1, d))


_C_QA = 0
_C_QB = _C_QA + A_WIDTH
_C_QI = _C_QB + B_WIDTH
_C_KV6 = _C_QI + A_IDX_HEADS * A_IDX_DIM
_C_LAT = _C_KV6 + 6 * B_KV_WIDTH
_C_KIW = _C_LAT + A_KV_RANK
_C_GB = _C_KIW + LANES
_P_MIX = _C_GB + LANES
_MISC_WI = 64


def _mixin_kernel(x_ref, mod_ref, w_ref, cos_ref, sin_ref, kvg_ref, wkv_ref,
                  qa_ref, qb_ref, qi_ref, ka_ref, va_ref, ki_ref, cmp_ref,
                  ks_ref, vs_ref, kw_ref, vw_ref, misc_ref, h_ref):
    x = x_ref[0]
    sh = mod_ref[0, 3:4, :]
    sc = mod_ref[0, 4:5, :]
    u = (x * (1.0 + sc) + sh).astype(BF16)
    h_ref[...] = _dot(u, w_ref[...])

    tm = x.shape[0]
    cos = cos_ref[...]
    sin = sin_ref[...]
    lane = lax.broadcasted_iota(I32, (tm, LANES), 1)
    first_half = (lane & (HEAD_DIM - 1)) < (HEAD_DIM // 2)

    def rope(t):
        rot = jnp.where(first_half, pltpu.roll(t, LANES - HEAD_DIM // 2, 1),
                        pltpu.roll(t, HEAD_DIM // 2, 1))
        return t * cos + rot * sin

    def chunk(c0):
        return h_ref[:, c0:c0 + LANES]

    scale = HEAD_DIM ** -0.5
    for c in range(A_WIDTH // LANES):
        t = (rope(chunk(_C_QA + c * LANES)) * scale).astype(BF16)
        qa_ref[0, 2 * c] = t[:, :HEAD_DIM]
        qa_ref[0, 2 * c + 1] = t[:, HEAD_DIM:]
    for c in range(B_WIDTH // LANES):
        t = (rope(chunk(_C_QB + c * LANES)) * scale).astype(BF16)
        qb_ref[0, 2 * c] = t[:, :HEAD_DIM]
        qb_ref[0, 2 * c + 1] = t[:, HEAD_DIM:]
    for c in range(A_IDX_HEADS * A_IDX_DIM // LANES):
        t = rope(chunk(_C_QI + c * LANES)).astype(BF16)
        qi_ref[0, 2 * c] = t[:, :A_IDX_DIM]
        qi_ref[0, 2 * c + 1] = t[:, A_IDX_DIM:]

    kc = rope(chunk(_C_KV6))
    vc = chunk(_C_KV6 + LANES)
    cmp_ref[0, :, :LANES] = kc
    cmp_ref[0, :, LANES:] = vc
    ks = rope(chunk(_C_KV6 + 2 * LANES)).astype(BF16)
    vs = chunk(_C_KV6 + 3 * LANES).astype(BF16)
    kw = rope(chunk(_C_KV6 + 4 * LANES)).astype(BF16)
    vw = chunk(_C_KV6 + 5 * LANES).astype(BF16)
    for g in range(B_GROUPS):
        sl = slice(g * HEAD_DIM, (g + 1) * HEAD_DIM)
        ks_ref[0, g] = ks[:, sl]
        vs_ref[0, g] = vs[:, sl]
        kw_ref[0, g] = kw[:, sl]
        vw_ref[0, g] = vw[:, sl]

    lat = chunk(_C_LAT)
    latn = lat * lax.rsqrt(jnp.mean(lat * lat, axis=-1, keepdims=True) + LN_EPS) * kvg_ref[...]
    kv = _dot(latn.astype(BF16), wkv_ref[...])
    ka_ref[0] = rope(kv)[:, :HEAD_DIM].astype(BF16)
    va_ref[0] = kv[:, HEAD_DIM:].astype(BF16)

    kiw = chunk(_C_KIW)
    ki_ref[0] = rope(kiw)[:, :A_IDX_DIM].astype(BF16)
    gb = jax.nn.sigmoid(chunk(_C_GB))
    wi = kiw * (A_IDX_HEADS ** -0.5)
    is_wi = (lane >= _MISC_WI) & (lane < _MISC_WI + A_IDX_HEADS)
    misc_ref[0] = jnp.where(lane < 3 * B_HEADS, gb, jnp.where(is_wi, wi, 0.0))


def _rope_tables(s):
    half = HEAD_DIM // 2
    inv = ROPE_THETA ** (-jnp.arange(half, dtype=F32) / half)
    ang = jnp.arange(s).astype(F32)[:, None] * inv[None, :]
    cos = jnp.cos(ang)
    sin = jnp.sin(ang)
    reps = LANES // HEAD_DIM
    cos_t = jnp.tile(jnp.concatenate([cos, cos], axis=-1), (1, reps))
    sin_t = jnp.tile(jnp.concatenate([-sin, sin], axis=-1), (1, reps))
    return cos_t, sin_t


def _mixin(x, mod, w_mix_in, kv_norm_g, w_kv_up):
    bsz, s, d = x.shape
    tm = TOKEN_TILE
    split = np.cumsum([A_WIDTH, A_KV_RANK, A_IDX_HEADS * A_IDX_DIM, A_IDX_DIM, A_IDX_HEADS,
                       B_WIDTH, 6 * B_KV_WIDTH, 3 * B_HEADS]).tolist()
    w_qa, w_lat, w_qi, w_ki, w_wi, w_qb, w_kv6, w_gb, _ = jnp.split(w_mix_in, split, axis=-1)
    zeros = lambda n: jnp.zeros((d, n), w_mix_in.dtype)
    w = jnp.concatenate(
        [w_qa, w_qb, w_qi, w_kv6, w_lat,
         w_ki, w_wi, zeros(LANES - A_IDX_DIM - A_IDX_HEADS),
         w_gb, zeros(LANES - 3 * B_HEADS)], axis=-1).astype(BF16)
    assert w.shape[1] == _P_MIX
    cos_t, sin_t = _rope_tables(s)
    tok = lambda width: pl.BlockSpec((1, tm, width), lambda b, i: (b, i, 0))
    heads = lambda n: pl.BlockSpec((1, n, tm, HEAD_DIM), lambda b, i: (b, 0, i, 0))
    sds = jax.ShapeDtypeStruct
    out_shape = [sds((bsz, A_HEADS, s, HEAD_DIM), BF16),
                 sds((bsz, B_HEADS, s, HEAD_DIM), BF16),
                 sds((bsz, A_IDX_HEADS, s, A_IDX_DIM), BF16),
                 sds((bsz, s, HEAD_DIM), BF16),
                 sds((bsz, s, HEAD_DIM), BF16),
                 sds((bsz, s, A_IDX_DIM), BF16),
                 sds((bsz, s, 2 * LANES), F32),
                 sds((bsz, B_GROUPS, s, HEAD_DIM), BF16),
                 sds((bsz, B_GROUPS, s, HEAD_DIM), BF16),
                 sds((bsz, B_GROUPS, s, HEAD_DIM), BF16),
                 sds((bsz, B_GROUPS, s, HEAD_DIM), BF16),
                 sds((bsz, s, LANES), F32)]
    out_specs = [heads(A_HEADS), heads(B_HEADS), heads(A_IDX_HEADS),
                 tok(HEAD_DIM), tok(HEAD_DIM), tok(A_IDX_DIM), tok(2 * LANES),
                 heads(B_GROUPS), heads(B_GROUPS), heads(B_GROUPS), heads(B_GROUPS), tok(LANES)]
    return pl.pallas_call(
        _mixin_kernel,
        grid=(bsz, s // tm),
        in_specs=[pl.BlockSpec((1, tm, d), lambda b, i: (b, i, 0)),
                  pl.BlockSpec((1, N_MOD, d), lambda b, i: (b, 0, 0)),
                  pl.BlockSpec((d, _P_MIX), lambda b, i: (0, 0)),
                  pl.BlockSpec((tm, LANES), lambda b, i: (i, 0)),
                  pl.BlockSpec((tm, LANES), lambda b, i: (i, 0)),
                  pl.BlockSpec((1, A_KV_RANK), lambda b, i: (0, 0)),
                  pl.BlockSpec((A_KV_RANK, 2 * HEAD_DIM), lambda b, i: (0, 0))],
        out_specs=out_specs,
        out_shape=out_shape,
        scratch_shapes=[pltpu.VMEM((tm, _P_MIX), F32)],
        compiler_params=_params(2),
        name="mixin",
    )(x, mod, w, cos_t, sin_t, kv_norm_g.reshape(1, A_KV_RANK), w_kv_up.astype(BF16))


def _online_softmax_step(s3, sel, v, m_ref, l_ref, acc_ref):
    nh, nq, nk = s3.shape
    sm = jnp.where(sel[None], s3, NEG)
    m_old = m_ref[...]
    m_new = jnp.maximum(m_old, jnp.max(sm, axis=-1, keepdims=True))
    p = jnp.where(sel[None], jnp.exp(sm - m_new), 0.0)
    alpha = jnp.exp(m_old - m_new)
    l_ref[...] = alpha * l_ref[...] + jnp.sum(p, axis=-1, keepdims=True)
    pv = _dot(p.reshape(nh * nq, nk).astype(BF16), v).reshape(nh, nq, v.shape[-1])
    acc_ref[...] = alpha * acc_ref[...] + pv
    m_ref[...] = m_new


def _masked_softmax(s3, mask):
    sm = jnp.where(mask[None], s3, NEG)
    e = jnp.exp(sm - jnp.max(sm, axis=-1, keepdims=True))
    p = e / jnp.sum(e, axis=-1, keepdims=True)
    return jnp.where(mask[None], p, 0.0)


def _dsa_kernel(qa_ref, qi_ref, misc_ref, ki_ref, ka_ref, va_ref, o_ref,
                key_ref, m_ref, l_ref, acc_ref, *, topk, idx_bits):
    qb = Q_BLOCK
    kc = KEY_CHUNK
    q0 = pl.program_id(1) * qb
    n_chunks = lax.div(q0 + qb + kc - 1, kc)
    tq = q0 + lax.broadcasted_iota(I32, (qb, 1), 0)
    lane_k = lax.broadcasted_iota(I32, (qb, kc), 1)
    wi = misc_ref[0][:, _MISC_WI:_MISC_WI + A_IDX_HEADS]

    def score_chunk(c, carry):
        k0 = pl.multiple_of(c * kc, kc)
        ki = ki_ref[0, pl.ds(k0, kc), :]
        acc = jnp.zeros((qb, kc), F32)
        for h in range(A_IDX_HEADS):
            acc = acc + wi[:, h:h + 1] * jnp.maximum(_dot_nt(qi_ref[0, h], ki), 0.0)
        sc = jnp.where(k0 + lane_k <= tq, acc, NEG)
        sc = jnp.where(sc == 0.0, 0.0, sc)
        bits = pltpu.bitcast(sc, I32)
        key_ref[:, pl.ds(k0, kc)] = jnp.where(bits < 0, bits ^ 0x7FFFFFFF, bits)
        return carry

    lax.fori_loop(0, n_chunks, score_chunk, 0)

    def count(pred):
        def body(c, acc):
            k0 = pl.multiple_of(c * kc, kc)
            hit = jnp.where(pred(key_ref[:, pl.ds(k0, kc)], k0 + lane_k), 1.0, 0.0)
            part = hit[:, :LANES]
            for i in range(1, kc // LANES):
                part = part + hit[:, i * LANES:(i + 1) * LANES]
            return acc + part
        acc = lax.fori_loop(0, n_chunks, body, jnp.zeros((qb, LANES), F32))
        return jnp.sum(acc, axis=-1, keepdims=True)

    def search(it, carry):
        thr, cnt_thr = carry
        cand = thr + lax.shift_left(jnp.int32(1), 31 - it)
        cnt = count(lambda key, pos: key >= cand)
        ok = cnt >= topk
        return jnp.where(ok, cand, thr), jnp.where(ok, cnt, cnt_thr)

    thr0 = jnp.full((qb, 1), INT_MIN, I32)
    cnt0 = jnp.full((qb, 1), 1.0, F32) * (n_chunks * kc).astype(F32)
    thr, cnt_thr = lax.fori_loop(0, 32, search, (thr0, cnt0))

    has_tie = jnp.max(jnp.where((cnt_thr > topk) & (thr > NEG_KEY), 1.0, 0.0)) > 0.0

    def tie_search():
        need = topk - count(lambda key, pos: key > thr)

        def step(it, last):
            cand = last + lax.shift_left(jnp.int32(1), idx_bits - 1 - it)
            below = count(lambda key, pos: (key == thr) & (pos < cand))
            return jnp.where(below < need, cand, last)

        return lax.fori_loop(0, idx_bits, step, jnp.zeros((qb, 1), I32))

    last_eq = lax.cond(has_tie, tie_search, lambda: jnp.full((qb, 1), INT_MAX, I32))

    m_ref[...] = jnp.full_like(m_ref, NEG)
    l_ref[...] = jnp.zeros_like(l_ref)
    acc_ref[...] = jnp.zeros_like(acc_ref)
    q_all = qa_ref[0].reshape(A_HEADS * qb, HEAD_DIM)

    def attend(c, carry):
        k0 = pl.multiple_of(c * kc, kc)
        key = key_ref[:, pl.ds(k0, kc)]
        pos = k0 + lane_k
        sel = ((key > thr) | ((key == thr) & (pos <= last_eq))) & (pos <= tq)
        s3 = _dot_nt(q_all, ka_ref[0, pl.ds(k0, kc), :]).reshape(A_HEADS, qb, kc)
        _online_softmax_step(s3, sel, va_ref[0, pl.ds(k0, kc), :], m_ref, l_ref, acc_ref)
        return carry

    lax.fori_loop(0, n_chunks, attend, 0)
    o = acc_ref[...] / l_ref[...]
    for h in range(A_HEADS):
        o_ref[0, :, h * HEAD_DIM:(h + 1) * HEAD_DIM] = o[h].astype(o_ref.dtype)


def _dsa(qa, qi, misc, ki, ka, va):
    bsz, _, s, _ = qa.shape
    qb = Q_BLOCK
    assert s % KEY_CHUNK == 0 and KEY_CHUNK >= A_TOPK_MAX
    topk = min(A_TOPK_MAX, s // 4)
    idx_bits = int(np.ceil(np.log2(s)))
    full = lambda width: pl.BlockSpec((1, s, width), lambda b, n: (b, 0, 0))
    return pl.pallas_call(
        functools.partial(_dsa_kernel, topk=topk, idx_bits=idx_bits),
        grid=(bsz, s // qb),
        in_specs=[pl.BlockSpec((1, A_HEADS, qb, HEAD_DIM), lambda b, n: (b, 0, n, 0)),
                  pl.BlockSpec((1, A_IDX_HEADS, qb, A_IDX_DIM), lambda b, n: (b, 0, n, 0)),
                  pl.BlockSpec((1, qb, LANES), lambda b, n: (b, n, 0)),
                  full(A_IDX_DIM), full(HEAD_DIM), full(HEAD_DIM)],
        out_specs=pl.BlockSpec((1, qb, A_WIDTH), lambda b, n: (b, n, 0)),
        out_shape=jax.ShapeDtypeStruct((bsz, s, A_WIDTH), BF16),
        scratch_shapes=[pltpu.VMEM((qb, s), I32),
                        pltpu.VMEM((A_HEADS, qb, 1), F32),
                        pltpu.VMEM((A_HEADS, qb, 1), F32),
                        pltpu.VMEM((A_HEADS, qb, HEAD_DIM), F32)],
        compiler_params=_params(2),
        name="dsa",
    )(qa, qi, misc, ki, ka, va)


def _compress_kernel(xk_ref, xv_ref, pek_ref, pev_ref, w1k_ref, w1v_ref, w2k_ref, w2v_ref,
                     ok_ref, ov_ref):
    def one(x_ref, pe_ref, w1_ref, w2_ref, o_ref):
        x = x_ref[0, 0]
        n = x.shape[0]
        nxt = pltpu.roll(x, n - 1, 0)
        a = (x + pe_ref[0:1, :]).astype(BF16)
        b = (nxt + pe_ref[1:2, :]).astype(BF16)
        hid = _dot(a, w1_ref[0]) + _dot(b, w1_ref[1])
        hid = (hid * jax.nn.sigmoid(hid)).astype(BF16)
        o_ref[0, 0] = _dot(hid, w2_ref[...]).astype(o_ref.dtype)

    one(xk_ref, pek_ref, w1k_ref, w2k_ref, ok_ref)
    one(xv_ref, pev_ref, w1v_ref, w2v_ref, ov_ref)


def _compress(cmp_in, pe_k, w1_k, w2_k, pe_v, w1_v, w2_v):
    bsz, s, _ = cmp_in.shape
    nch = s // B_CMP_STRIDE
    cw = B_CMP_STRIDE * HEAD_DIM

    def chunks(t):
        t = t.reshape(bsz, nch, B_CMP_STRIDE, B_GROUPS, HEAD_DIM)
        return t.transpose(0, 3, 1, 2, 4).reshape(bsz, B_GROUPS, nch, cw)

    xk = chunks(cmp_in[:, :, :LANES])
    xv = chunks(cmp_in[:, :, LANES:])
    xspec = pl.BlockSpec((1, 1, nch, cw), lambda b, g: (b, g, 0, 0))
    c2 = lambda b, g: (0, 0)
    c3 = lambda b, g: (0, 0, 0)
    ospec = pl.BlockSpec((1, 1, nch, HEAD_DIM), lambda b, g: (b, g, 0, 0))
    osd = jax.ShapeDtypeStruct((bsz, B_GROUPS, nch, HEAD_DIM), BF16)
    return pl.pallas_call(
        _compress_kernel,
        grid=(bsz, B_GROUPS),
        in_specs=[xspec, xspec,
                  pl.BlockSpec((2, cw), c2), pl.BlockSpec((2, cw), c2),
                  pl.BlockSpec((2, cw, B_CMP_HIDDEN), c3), pl.BlockSpec((2, cw, B_CMP_HIDDEN), c3),
                  pl.BlockSpec((B_CMP_HIDDEN, HEAD_DIM), c2), pl.BlockSpec((B_CMP_HIDDEN, HEAD_DIM), c2)],
        out_specs=[ospec, ospec],
        out_shape=[osd, osd],
        compiler_params=_params(2),
        name="compress",
    )(xk, xv, pe_k.reshape(2, cw), pe_v.reshape(2, cw),
      w1_k.astype(BF16).reshape(2, cw, B_CMP_HIDDEN), w1_v.astype(BF16).reshape(2, cw, B_CMP_HIDDEN),
      w2_k.astype(BF16), w2_v.astype(BF16))


def _nsa_kernel(qb_ref, misc_ref, kcmp_ref, vcmp_ref, ks_ref, vs_ref, kw_ref, vw_ref, ovl_ref,
                o_ref, m_ref, l_ref, acc_ref, *, n_slc, top_n):
    qb = Q_BLOCK
    kc = KEY_CHUNK
    rep = B_REP
    q0 = pl.program_id(1) * qb
    n_chunks = lax.div(q0 + qb + kc - 1, kc)
    tq = q0 + lax.broadcasted_iota(I32, (qb, 1), 0)
    gates = misc_ref[0]
    n_cmp = kcmp_ref.shape[2]
    cmp_end = lax.broadcasted_iota(I32, (qb, n_cmp), 1) * B_CMP_STRIDE + (B_CMP_LEN - 1)
    mask_c = cmp_end <= tq
    jj = lax.broadcasted_iota(I32, (qb, LANES), 1)
    jj_f = jj.astype(F32)
    blk_q = lax.shift_right_logical(tq, int(np.log2(B_SLC_LEN)))
    forced = (jj == 0) | (jj == blk_q) | (jj == blk_q - 1)
    lane_k = lax.broadcasted_iota(I32, (qb, kc), 1)
    blk_row = lax.broadcasted_iota(I32, (LANES, kc), 0)
    blk_lane = lax.broadcasted_iota(I32, (LANES, kc), 1)
    w0 = jnp.maximum(q0 - B_WINDOW, 0)
    wlen = B_WINDOW + qb
    pos_w = w0 + lax.broadcasted_iota(I32, (qb, wlen), 1)
    mask_w = (pos_w <= tq) & (pos_w > tq - B_WINDOW)

    for g in range(B_GROUPS):
        q4 = qb_ref[0, rep * g:rep * (g + 1)].reshape(rep * qb, HEAD_DIM)

        s_c = _dot_nt(q4, kcmp_ref[0, g]).reshape(rep, qb, n_cmp)
        p_c = _masked_softmax(s_c, mask_c)
        o_c = _dot(p_c.reshape(rep * qb, n_cmp).astype(BF16), vcmp_ref[0, g]).reshape(rep, qb, HEAD_DIM)

        p_sum = p_c[0]
        for r in range(1, rep):
            p_sum = p_sum + p_c[r]
        hi = p_sum.astype(BF16)
        r1 = p_sum - hi.astype(F32)
        mid = r1.astype(BF16)
        lo = (r1 - mid.astype(F32)).astype(BF16)
        ovl = ovl_ref[...]
        imp = _dot(hi, ovl) + _dot(mid, ovl) + _dot(lo, ovl)
        val = jnp.where(jj <= blk_q, imp + FORCE_BONUS * jnp.where(forced, 1.0, 0.0), NEG)
        val = jnp.where(jj < n_slc, val, -jnp.inf)

        def pick(i, carry):
            cur, chosen = carry
            mx = jnp.max(cur, axis=-1, keepdims=True)
            first = jnp.min(jnp.where(cur == mx, jj_f, float(LANES)), axis=-1, keepdims=True)
            hit = jj_f == first
            return jnp.where(hit, -jnp.inf, cur), jnp.where(hit, 1.0, chosen)

        _, chosen = lax.fori_loop(0, top_n, pick, (val, jnp.zeros((qb, LANES), F32)))
        chosen = chosen.astype(BF16)

        m_ref[...] = jnp.full_like(m_ref, NEG)
        l_ref[...] = jnp.zeros_like(l_ref)
        acc_ref[...] = jnp.zeros_like(acc_ref)

        def attend(c, carry):
            k0 = pl.multiple_of(c * kc, kc)
            expand = jnp.where(blk_row == lax.shift_right_logical(k0 + blk_lane, int(np.log2(B_SLC_LEN))),
                               1.0, 0.0).astype(BF16)
            sel = (_dot(chosen, expand) > 0.5) & (k0 + lane_k <= tq)
            s3 = _dot_nt(q4, ks_ref[0, g, pl.ds(k0, kc), :]).reshape(rep, qb, kc)
            _online_softmax_step(s3, sel, vs_ref[0, g, pl.ds(k0, kc), :], m_ref, l_ref, acc_ref)
            return carry

        lax.fori_loop(0, n_chunks, attend, 0)
        o_s = acc_ref[...] / l_ref[...]

        w0a = pl.multiple_of(w0, qb)
        s_w = _dot_nt(q4, kw_ref[0, g, pl.ds(w0a, wlen), :]).reshape(rep, qb, wlen)
        p_w = _masked_softmax(s_w, mask_w)
        o_w = _dot(p_w.reshape(rep * qb, wlen).astype(BF16),
                   vw_ref[0, g, pl.ds(w0a, wlen), :]).reshape(rep, qb, HEAD_DIM)

        for r in range(rep):
            h = g * rep + r
            o = (gates[:, h:h + 1] * o_c[r] + gates[:, B_HEADS + h:B_HEADS + h + 1] * o_s[r]
                 + gates[:, 2 * B_HEADS + h:2 * B_HEADS + h + 1] * o_w[r])
            o_ref[0, :, h * HEAD_DIM:(h + 1) * HEAD_DIM] = o.astype(o_ref.dtype)


def _nsa(qb, misc, kcmp, vcmp, ks, vs, kw, vw):
    bsz, _, s, _ = qb.shape
    qblk = Q_BLOCK
    n_cmp = kcmp.shape[2]
    n_slc = s // B_SLC_LEN
    top_n = min(B_SLC_TOP, n_slc)
    assert s % KEY_CHUNK == 0 and s >= B_WINDOW + qblk and n_slc <= LANES
    cmp_start = np.arange(n_cmp)[:, None] * B_CMP_STRIDE
    slc_start = np.arange(LANES)[None, :] * B_SLC_LEN
    overlap = ((cmp_start < slc_start + B_SLC_LEN) & (slc_start < cmp_start + B_CMP_LEN)
               & (np.arange(LANES)[None, :] < n_slc))
    overlap = jnp.asarray(overlap, BF16)
    cmp_spec = pl.BlockSpec((1, B_GROUPS, n_cmp, HEAD_DIM), lambda b, n: (b, 0, 0, 0))
    kv_spec = pl.BlockSpec((1, B_GROUPS, s, HEAD_DIM), lambda b, n: (b, 0, 0, 0))
    return pl.pallas_call(
        functools.partial(_nsa_kernel, n_slc=n_slc, top_n=top_n),
        grid=(bsz, s // qblk),
        in_specs=[pl.BlockSpec((1, B_HEADS, qblk, HEAD_DIM), lambda b, n: (b, 0, n, 0)),
                  pl.BlockSpec((1, qblk, LANES), lambda b, n: (b, n, 0)),
                  cmp_spec, cmp_spec, kv_spec, kv_spec, kv_spec, kv_spec,
                  pl.BlockSpec((n_cmp, LANES), lambda b, n: (0, 0))],
        out_specs=pl.BlockSpec((1, qblk, B_WIDTH), lambda b, n: (b, n, 0)),
        out_shape=jax.ShapeDtypeStruct((bsz, s, B_WIDTH), BF16),
        scratch_shapes=[pltpu.VMEM((B_REP, qblk, 1), F32),
                        pltpu.VMEM((B_REP, qblk, 1), F32),
                        pltpu.VMEM((B_REP, qblk, HEAD_DIM), F32)],
        compiler_params=_params(2),
        name="nsa",
    )(qb, misc, kcmp, vcmp, ks, vs, kw, vw, overlap)


def _merge_kernel(x_ref, mod_ref, ya_ref, yb_ref, wgm_ref, wa_ref, wb_ref, wo_ref, g_ref, b_ref,
                  o_ref, *, alpha):
    x = x_ref[0]
    d = x.shape[-1]
    sh = mod_ref[0, 3:4, :]
    sc = mod_ref[0, 4:5, :]
    gt = mod_ref[0, 5:6, :]
    u = (x * (1.0 + sc) + sh).astype(BF16)
    gm = jax.nn.sigmoid(_dot(u, wgm_ref[...]))
    merged = gm[:, :d] * _dot(ya_ref[0], wa_ref[...]) + gm[:, d:] * _dot(yb_ref[0], wb_ref[...])
    mix = _dot(merged.astype(BF16), wo_ref[...])
    o_ref[0] = _layer_norm(alpha * x + gt * mix, g_ref[...], b_ref[...])


def _merge(x, mod, ya, yb, w_gm, w_br_a, w_br_b, w_mix_out, ln_g, ln_b, alpha):
    bsz, s, d = x.shape
    tm = TOKEN_TILE
    c2 = lambda b, i: (0, 0)
    tok = lambda width: pl.BlockSpec((1, tm, width), lambda b, i: (b, i, 0))
    return pl.pallas_call(
        functools.partial(_merge_kernel, alpha=alpha),
        grid=(bsz, s // tm),
        in_specs=[tok(d),
                  pl.BlockSpec((1, N_MOD, d), lambda b, i: (b, 0, 0)),
                  tok(A_WIDTH), tok(B_WIDTH),
                  pl.BlockSpec((d, 2 * d), c2),
                  pl.BlockSpec((A_WIDTH, d), c2),
                  pl.BlockSpec((B_WIDTH, d), c2),
                  pl.BlockSpec((d, d), c2),
                  pl.BlockSpec((1, d), c2), pl.BlockSpec((1, d), c2)],
        out_specs=tok(d),
        out_shape=jax.ShapeDtypeStruct((bsz, s, d), F32),
        compiler_params=_params(2),
        name="merge",
    )(x, mod, ya, yb, w_gm.astype(BF16), w_br_a.astype(BF16), w_br_b.astype(BF16),
      w_mix_out.astype(BF16), ln_g.reshape(1, d), ln_b.reshape(1, d))


def kernel(x, c, w_ada, b_ada, ffn1_w_in, ffn1_w_out, ln1_g, ln1_b, w_mix_in, kv_norm_g, w_kv_up,
           cmp_pe_k, cmp_w1_k, cmp_w2_k, cmp_pe_v, cmp_w1_v, cmp_w2_v, w_br_a, w_br_b, w_mix_out,
           ln2_g, ln2_b, ffn2_w_in, ffn2_w_out, ln3_g, ln3_b):
    bsz, s, d = x.shape
    depth = w_ada.shape[0]
    alpha = (2.0 * depth) ** 0.25
    assert s % TOKEN_TILE == 0 and s % KEY_CHUNK == 0
    for l in range(depth):
        mod = _ada(c, w_ada[l], b_ada[l]).reshape(bsz, N_MOD, d)
        x = _ffn(x, mod, ffn1_w_in[l], ffn1_w_out[l], ln1_g[l], ln1_b[l], 0, alpha)
        qa, qb, qi, ka, va, ki, cmp_in, ks, vs, kw, vw, misc = _mixin(
            x, mod, w_mix_in[l], kv_norm_g[l], w_kv_up[l])
        ya = _dsa(qa, qi, misc, ki, ka, va)
        kcmp, vcmp = _compress(cmp_in, cmp_pe_k[l], cmp_w1_k[l], cmp_w2_k[l],
                               cmp_pe_v[l], cmp_w1_v[l], cmp_w2_v[l])
        yb = _nsa(qb, misc, kcmp, vcmp, ks, vs, kw, vw)
        w_gm = w_mix_in[l][:, -2 * d:]
        x = _merge(x, mod, ya, yb, w_gm, w_br_a[l], w_br_b[l], w_mix_out[l], ln2_g[l], ln2_b[l], alpha)
        x = _ffn(x, mod, ffn2_w_in[l], ffn2_w_out[l], ln3_g[l], ln3_b[l], 6, alpha)
    return x
```

```python
import functools

import numpy as np
import jax
import jax.numpy as jnp
from jax import lax
from jax.experimental import pallas as pl
from jax.experimental.pallas import tpu as pltpu

F32 = jnp.float32
BF16 = jnp.bfloat16
I32 = jnp.int32

HEAD_DIM = 64
ROPE_THETA = 10000.0
A_HEADS = 8
A_WIDTH = A_HEADS * HEAD_DIM
A_KV_RANK = 128
A_IDX_HEADS = 4
A_IDX_DIM = 64
A_TOPK_MAX = 256
B_HEADS = 8
B_GROUPS = 2
B_REP = B_HEADS // B_GROUPS
B_WIDTH = B_HEADS * HEAD_DIM
B_KV_WIDTH = B_GROUPS * HEAD_DIM
B_CMP_STRIDE = 16
B_CMP_LEN = 2 * B_CMP_STRIDE
B_CMP_HIDDEN = 256
B_SLC_LEN = 64
B_SLC_TOP = 16
B_WINDOW = 512
N_MOD = 9
LN_EPS = 1e-5
NEG = -1e30
FORCE_BONUS = 1e4

LANES = 128
SUBLANES = 8
Q_BLOCK = LANES
KEY_CHUNK = 512
TOKEN_TILE = 512
FF_CHUNK = 256
VMEM_LIMIT = 56 * 1024 * 1024

NEG_KEY = int(np.array(NEG, np.float32).view(np.int32)) ^ 0x7FFFFFFF
INT_MIN = -(2 ** 31)
INT_MAX = 2 ** 31 - 1


def _dot(a, b):
    return jnp.dot(a, b, preferred_element_type=F32)


def _dot_nt(a, b):
    return lax.dot_general(a, b, (((1,), (1,)), ((), ())), preferred_element_type=F32)


def _layer_norm(y, g, b):
    mu = jnp.mean(y, axis=-1, keepdims=True)
    d = y - mu
    var = jnp.mean(d * d, axis=-1, keepdims=True)
    return d * lax.rsqrt(var + LN_EPS) * g + b


def _params(n_axes):
    return pltpu.CompilerParams(dimension_semantics=("arbitrary",) * n_axes,
                                vmem_limit_bytes=VMEM_LIMIT)


def _ada_kernel(c_ref, w_ref, b_ref, o_ref):
    c = c_ref[...]
    ca = c * jax.nn.sigmoid(c)
    o_ref[...] = jnp.dot(ca, w_ref[...], preferred_element_type=F32,
                         precision=lax.Precision.HIGHEST) + b_ref[...]


def _ada(c, w_ada, b_ada):
    bsz, d = c.shape
    n = w_ada.shape[1]
    tn = n // 8
    return pl.pallas_call(
        _ada_kernel,
        grid=(n // tn,),
        in_specs=[pl.BlockSpec((bsz, d), lambda j: (0, 0)),
                  pl.BlockSpec((d, tn), lambda j: (0, j)),
                  pl.BlockSpec((1, tn), lambda j: (0, j))],
        out_specs=pl.BlockSpec((bsz, tn), lambda j: (0, j)),
        out_shape=jax.ShapeDtypeStruct((bsz, n), F32),
        compiler_params=_params(1),
        name="ada",
    )(c, w_ada, b_ada.reshape(1, n))


def _ffn_kernel(x_ref, mod_ref, wa_ref, wb_ref, wo_ref, g_ref, b_ref, o_ref, u_ref, acc_ref,
                *, mod_base, n_chunks, alpha):
    x = x_ref[0]
    sh = mod_ref[0, mod_base:mod_base + 1, :]
    sc = mod_ref[0, mod_base + 1:mod_base + 2, :]
    gt = mod_ref[0, mod_base + 2:mod_base + 3, :]
    u_ref[...] = (x * (1.0 + sc) + sh).astype(BF16)
    acc_ref[...] = jnp.zeros_like(acc_ref)

    def body(j, carry):
        u = u_ref[...]
        a = _dot(u, wa_ref[j])
        b = _dot(u, wb_ref[j])
        h = (a * jax.nn.sigmoid(a) * b).astype(BF16)
        acc_ref[...] += _dot(h, wo_ref[j])
        return carry

    lax.fori_loop(0, n_chunks, body, 0)
    y = alpha * x + (0.5 * gt) * acc_ref[...]
    o_ref[0] = _layer_norm(y, g_ref[...], b_ref[...])


def _ffn(x, mod, w_in, w_out, ln_g, ln_b, mod_base, alpha):
    bsz, s, d = x.shape
    d_ff = w_out.shape[0]
    n_chunks = d_ff // FF_CHUNK
    tm = TOKEN_TILE
    wa = w_in[:, :d_ff].astype(BF16).reshape(d, n_chunks, FF_CHUNK).transpose(1, 0, 2)
    wb = w_in[:, d_ff:].astype(BF16).reshape(d, n_chunks, FF_CHUNK).transpose(1, 0, 2)
    wo = w_out.astype(BF16).reshape(n_chunks, FF_CHUNK, d)
    const3 = lambda b, i: (0, 0, 0)
    return pl.pallas_call(
        functools.partial(_ffn_kernel, mod_base=mod_base, n_chunks=n_chunks, alpha=alpha),
        grid=(bsz, s // tm),
        in_specs=[pl.BlockSpec((1, tm, d), lambda b, i: (b, i, 0)),
                  pl.BlockSpec((1, N_MOD, d), lambda b, i: (b, 0, 0)),
                  pl.BlockSpec((n_chunks, d, FF_CHUNK), const3),
                  pl.BlockSpec((n_chunks, d, FF_CHUNK), const3),
                  pl.BlockSpec((n_chunks, FF_CHUNK, d), const3),
                  pl.BlockSpec((1, d), lambda b, i: (0, 0)),
                  pl.BlockSpec((1, d), lambda b, i: (0, 0))],
        out_specs=pl.BlockSpec((1, tm, d), lambda b, i: (b, i, 0)),
        out_shape=jax.ShapeDtypeStruct((bsz, s, d), F32),
        scratch_shapes=[pltpu.VMEM((tm, d), BF16), pltpu.VMEM((tm, d), F32)],
        compiler_params=_params(2),
        name="ffn",
    )(x, mod, wa, wb, wo, ln_g.reshape(1, d), ln_b.reshape(1, d))


_S_KC = 0
_S_VC = _S_KC + B_KV_WIDTH
_S_KS = _S_VC + B_KV_WIDTH
_S_KW = _S_KS + B_KV_WIDTH
_S_LAT = _S_KW + B_KV_WIDTH
_S_KI = _S_LAT + A_KV_RANK
_P_STD = _S_KI + LANES
_T_QA = 0
_T_QB = _T_QA + A_WIDTH
_T_QI = _T_QB + B_WIDTH
_T_VS = _T_QI + A_IDX_HEADS * A_IDX_DIM
_T_VW = _T_VS + B_KV_WIDTH
_T_MISC = _T_VW + B_KV_WIDTH
_MISC_ROWS = 32
_MISC_WI = 3 * B_HEADS
_P_T = _T_MISC + _MISC_ROWS


def _mixin_kernel(x_ref, mod_ref, ws_ref, wt_ref, cos_ref, sin_ref, cost_ref, sint_ref,
                  kvg_ref, wk_ref, wvt_ref,
                  qat_ref, qbt_ref, qit_ref, vat_ref, vst_ref, vwt_ref, misct_ref,
                  ka_ref, ki_ref, ks_ref, kw_ref, cmp_ref, hs_ref, ht_ref):
    x = x_ref[0]
    sh = mod_ref[0, 3:4, :]
    sc = mod_ref[0, 4:5, :]
    u = (x * (1.0 + sc) + sh).astype(BF16)
    hs_ref[...] = _dot(u, ws_ref[...])
    ht_ref[...] = _dot_nt(wt_ref[...], u)

    tm = x.shape[0]
    half = HEAD_DIM // 2
    scale = HEAD_DIM ** -0.5

    cos_t = cost_ref[...]
    sin_t = sint_ref[...]

    def rope_t(src0, dst_ref, dst0, mul):
        x1 = ht_ref[src0:src0 + half, :]
        x2 = ht_ref[src0 + half:src0 + HEAD_DIM, :]
        o1 = x1 * cos_t - x2 * sin_t
        o2 = x1 * sin_t + x2 * cos_t
        if mul != 1.0:
            o1 = o1 * mul
            o2 = o2 * mul
        dst_ref[0, dst0:dst0 + half, :] = o1.astype(dst_ref.dtype)
        dst_ref[0, dst0 + half:dst0 + HEAD_DIM, :] = o2.astype(dst_ref.dtype)

    for h in range(A_HEADS):
        rope_t(_T_QA + h * HEAD_DIM, qat_ref, h * HEAD_DIM, scale)
    for h in range(B_HEADS):
        rope_t(_T_QB + h * HEAD_DIM, qbt_ref, h * HEAD_DIM, scale)
    for h in range(A_IDX_HEADS):
        rope_t(_T_QI + h * A_IDX_DIM, qit_ref, h * A_IDX_DIM, 1.0)
    vst_ref[0] = ht_ref[_T_VS:_T_VS + B_KV_WIDTH, :].astype(BF16)
    vwt_ref[0] = ht_ref[_T_VW:_T_VW + B_KV_WIDTH, :].astype(BF16)
    misc = ht_ref[_T_MISC:_T_MISC + _MISC_ROWS, :]
    row = lax.broadcasted_iota(I32, misc.shape, 0)
    misct_ref[0] = jnp.where(row < _MISC_WI, jax.nn.sigmoid(misc), misc * (A_IDX_HEADS ** -0.5))

    cos = cos_ref[...]
    sin = sin_ref[...]
    lane = lax.broadcasted_iota(I32, (tm, LANES), 1)
    first_half = (lane & (HEAD_DIM - 1)) < half

    def rope(t):
        rot = jnp.where(first_half, pltpu.roll(t, LANES - half, 1), pltpu.roll(t, half, 1))
        return t * cos + rot * sin

    def chunk(c0):
        return hs_ref[:, c0:c0 + LANES]

    cmp_ref[0, :, :LANES] = rope(chunk(_S_KC))
    cmp_ref[0, :, LANES:] = chunk(_S_VC)
    ks = rope(chunk(_S_KS)).astype(BF16)
    kw = rope(chunk(_S_KW)).astype(BF16)
    for g in range(B_GROUPS):
        ks_ref[0, g] = ks[:, g * HEAD_DIM:(g + 1) * HEAD_DIM]
        kw_ref[0, g] = kw[:, g * HEAD_DIM:(g + 1) * HEAD_DIM]
    ki_ref[0] = rope(chunk(_S_KI))[:, :A_IDX_DIM].astype(BF16)

    lat = chunk(_S_LAT)
    latn = (lat * lax.rsqrt(jnp.mean(lat * lat, axis=-1, keepdims=True) + LN_EPS)
            * kvg_ref[...]).astype(BF16)
    ka_ref[0] = rope(_dot(latn, wk_ref[...]))[:, :HEAD_DIM].astype(BF16)
    vat_ref[0] = _dot_nt(wvt_ref[...], latn).astype(BF16)


def _rope_tables(s):
    half = HEAD_DIM // 2
    inv = ROPE_THETA ** (-jnp.arange(half, dtype=F32) / half)
    ang = jnp.arange(s).astype(F32)[:, None] * inv[None, :]
    cos = jnp.cos(ang)
    sin = jnp.sin(ang)
    reps = LANES // HEAD_DIM
    cos_l = jnp.tile(jnp.concatenate([cos, cos], axis=-1), (1, reps))
    sin_l = jnp.tile(jnp.concatenate([-sin, sin], axis=-1), (1, reps))
    return cos_l, sin_l, cos.T, sin.T


def _mixin(x, mod, w_mix_in, kv_norm_g, w_kv_up):
    bsz, s, d = x.shape
    tm = TOKEN_TILE
    split = np.cumsum([A_WIDTH, A_KV_RANK, A_IDX_HEADS * A_IDX_DIM, A_IDX_DIM, A_IDX_HEADS,
                       B_WIDTH, 6 * B_KV_WIDTH, 3 * B_HEADS]).tolist()
    w_qa, w_lat, w_qi, w_ki, w_wi, w_qb, w_kv6, w_gb, _ = jnp.split(w_mix_in, split, axis=-1)
    w_kc, w_vc, w_ks, w_vs, w_kw, w_vw = jnp.split(w_kv6, 6, axis=-1)
    zeros = lambda n: jnp.zeros((d, n), w_mix_in.dtype)
    w_std = jnp.concatenate([w_kc, w_vc, w_ks, w_kw, w_lat, w_ki, zeros(LANES - A_IDX_DIM)],
                            axis=-1).astype(BF16)
    w_t = jnp.concatenate([w_qa, w_qb, w_qi, w_vs, w_vw, w_gb, w_wi,
                           zeros(_MISC_ROWS - 3 * B_HEADS - A_IDX_HEADS)], axis=-1).T.astype(BF16)
    assert w_std.shape[1] == _P_STD and w_t.shape[0] == _P_T
    cos_l, sin_l, cos_t, sin_t = _rope_tables(s)
    w_k = jnp.concatenate([w_kv_up[:, :HEAD_DIM], jnp.zeros((A_KV_RANK, LANES - HEAD_DIM), F32)],
                          axis=-1).astype(BF16)
    w_vt = w_kv_up[:, HEAD_DIM:].T.astype(BF16)
    c2 = lambda b, i: (0, 0)
    feat = lambda rows: pl.BlockSpec((1, rows, tm), lambda b, i: (b, 0, i))
    tok = lambda width: pl.BlockSpec((1, tm, width), lambda b, i: (b, i, 0))
    grp = pl.BlockSpec((1, B_GROUPS, tm, HEAD_DIM), lambda b, i: (b, 0, i, 0))
    sds = jax.ShapeDtypeStruct
    out_shape = [sds((bsz, A_WIDTH, s), BF16),
                 sds((bsz, B_WIDTH, s), BF16),
                 sds((bsz, A_IDX_HEADS * A_IDX_DIM, s), BF16),
                 sds((bsz, HEAD_DIM, s), BF16),
                 sds((bsz, B_KV_WIDTH, s), BF16),
                 sds((bsz, B_KV_WIDTH, s), BF16),
                 sds((bsz, _MISC_ROWS, s), F32),
                 sds((bsz, s, HEAD_DIM), BF16),
                 sds((bsz, s, A_IDX_DIM), BF16),
                 sds((bsz, B_GROUPS, s, HEAD_DIM), BF16),
                 sds((bsz, B_GROUPS, s, HEAD_DIM), BF16),
                 sds((bsz, s, 2 * LANES), F32)]
    out_specs = [feat(A_WIDTH), feat(B_WIDTH), feat(A_IDX_HEADS * A_IDX_DIM), feat(HEAD_DIM),
                 feat(B_KV_WIDTH), feat(B_KV_WIDTH), feat(_MISC_ROWS),
                 tok(HEAD_DIM), tok(A_IDX_DIM), grp, grp, tok(2 * LANES)]
    return pl.pallas_call(
        _mixin_kernel,
        grid=(bsz, s // tm),
        in_specs=[pl.BlockSpec((1, tm, d), lambda b, i: (b, i, 0)),
                  pl.BlockSpec((1, N_MOD, d), lambda b, i: (b, 0, 0)),
                  pl.BlockSpec((d, _P_STD), c2),
                  pl.BlockSpec((_P_T, d), c2),
                  pl.BlockSpec((tm, LANES), lambda b, i: (i, 0)),
                  pl.BlockSpec((tm, LANES), lambda b, i: (i, 0)),
                  pl.BlockSpec((HEAD_DIM // 2, tm), lambda b, i: (0, i)),
                  pl.BlockSpec((HEAD_DIM // 2, tm), lambda b, i: (0, i)),
                  pl.BlockSpec((1, A_KV_RANK), c2),
                  pl.BlockSpec((A_KV_RANK, LANES), c2),
                  pl.BlockSpec((HEAD_DIM, A_KV_RANK), c2)],
        out_specs=out_specs,
        out_shape=out_shape,
        scratch_shapes=[pltpu.VMEM((tm, _P_STD), F32), pltpu.VMEM((_P_T, tm), F32)],
        compiler_params=_params(2),
        name="mixin",
    )(x, mod, w_std, w_t, cos_l, sin_l, cos_t, sin_t,
      kv_norm_g.reshape(1, A_KV_RANK), w_k, w_vt)


def _heads_on_lanes(qt, n_heads, dim):
    return jnp.concatenate([qt[h * dim:(h + 1) * dim, :] for h in range(n_heads)], axis=1)


def _fold_rows(x, op):
    while x.shape[0] % (2 * SUBLANES) == 0:
        h = x.shape[0] // 2
        x = op(x[:h], x[h:])
    return x


def _max_rows(x):
    return jnp.max(_fold_rows(x, jnp.maximum), axis=0, keepdims=True)


def _sum_rows(x):
    return jnp.sum(_fold_rows(x, jnp.add), axis=0, keepdims=True)


def _online_softmax_step_t(s_t, sel, v_t, m_ref, l_ref, acc_ref, n_heads):
    q = sel.shape[1]
    ps, alphas = [], []
    for h in range(n_heads):
        sl = slice(h * q, (h + 1) * q)
        sm = jnp.where(sel, s_t[:, sl], NEG)
        m_old = m_ref[:, sl]
        m_new = jnp.maximum(m_old, _max_rows(sm))
        p = jnp.where(sel, jnp.exp(sm - m_new), 0.0)
        alpha = jnp.exp(m_old - m_new)
        l_ref[:, sl] = alpha * l_ref[:, sl] + _sum_rows(p)
        m_ref[:, sl] = m_new
        ps.append(p.astype(BF16))
        alphas.append(alpha)
    acc_ref[...] = (jnp.concatenate(alphas, axis=1) * acc_ref[...]
                    + _dot(v_t, jnp.concatenate(ps, axis=1)))


def _masked_softmax_t(s, mask):
    sm = jnp.where(mask, s, NEG)
    e = jnp.exp(sm - _max_rows(sm))
    p = e / _sum_rows(e)
    return jnp.where(mask, p, 0.0)


def _dsa_kernel(qat_ref, qit_ref, misct_ref, ki_ref, ka_ref, vat_ref, o_ref,
                key_ref, m_ref, l_ref, acc_ref, *, topk, idx_bits):
    qb = Q_BLOCK
    kc = KEY_CHUNK
    q0 = pl.program_id(1) * qb
    n_chunks = lax.div(q0 + qb + kc - 1, kc)
    tq = q0 + lax.broadcasted_iota(I32, (1, qb), 1)
    row_k = lax.broadcasted_iota(I32, (kc, qb), 0)
    wi = misct_ref[0, _MISC_WI:_MISC_WI + A_IDX_HEADS, :]
    qi_t = qit_ref[0]
    qi_pairs = [_heads_on_lanes(qi_t[p * 2 * A_IDX_DIM:(p + 1) * 2 * A_IDX_DIM], 2, A_IDX_DIM)
                for p in range(A_IDX_HEADS // 2)]

    def score_chunk(c, carry):
        k0 = pl.multiple_of(c * kc, kc)
        ki = ki_ref[0, pl.ds(k0, kc), :]
        acc = jnp.zeros((kc, qb), F32)
        for p in range(A_IDX_HEADS // 2):
            lg = _dot(ki, qi_pairs[p])
            for j in range(2):
                h = 2 * p + j
                acc = acc + wi[h:h + 1, :] * jnp.maximum(lg[:, j * qb:(j + 1) * qb], 0.0)
        sc = jnp.where(k0 + row_k <= tq, acc, NEG)
        sc = jnp.where(sc == 0.0, 0.0, sc)
        bits = pltpu.bitcast(sc, I32)
        key_ref[pl.ds(k0, kc), :] = jnp.where(bits < 0, bits ^ 0x7FFFFFFF, bits)
        return carry

    lax.fori_loop(0, n_chunks, score_chunk, 0)

    def count(pred):
        def body(c, acc):
            k0 = pl.multiple_of(c * kc, kc)
            hit = jnp.where(pred(key_ref[pl.ds(k0, kc), :], k0 + row_k), 1.0, 0.0)
            return acc + _fold_rows(hit, jnp.add)
        acc = lax.fori_loop(0, n_chunks, body, jnp.zeros((SUBLANES, qb), F32))
        return jnp.sum(acc, axis=0, keepdims=True)

    def search(it, carry):
        thr, cnt_thr = carry
        cand = thr + lax.shift_left(jnp.int32(1), 31 - it)
        cnt = count(lambda key, pos: key >= cand)
        ok = cnt >= topk
        return jnp.where(ok, cand, thr), jnp.where(ok, cnt, cnt_thr)

    thr0 = jnp.full((1, qb), INT_MIN, I32)
    cnt0 = jnp.full((1, qb), 1.0, F32) * (n_chunks * kc).astype(F32)
    thr, cnt_thr = lax.fori_loop(0, 32, search, (thr0, cnt0))

    has_tie = jnp.max(jnp.where((cnt_thr > topk) & (thr > NEG_KEY), 1.0, 0.0)) > 0.0

    def tie_search():
        need = topk - count(lambda key, pos: key > thr)

        def step(it, last):
            cand = last + lax.shift_left(jnp.int32(1), idx_bits - 1 - it)
            below = count(lambda key, pos: (key == thr) & (pos < cand))
            return jnp.where(below < need, cand, last)

        return lax.fori_loop(0, idx_bits, step, jnp.zeros((1, qb), I32))

    last_eq = lax.cond(has_tie, tie_search, lambda: jnp.full((1, qb), INT_MAX, I32))

    m_ref[...] = jnp.full_like(m_ref, NEG)
    l_ref[...] = jnp.zeros_like(l_ref)
    acc_ref[...] = jnp.zeros_like(acc_ref)
    q_all = _heads_on_lanes(qat_ref[0], A_HEADS, HEAD_DIM)

    def attend(c, carry):
        k0 = pl.multiple_of(c * kc, kc)
        key = key_ref[pl.ds(k0, kc), :]
        pos = k0 + row_k
        sel = ((key > thr) | ((key == thr) & (pos <= last_eq))) & (pos <= tq)
        s_t = _dot(ka_ref[0, pl.ds(k0, kc), :], q_all)
        _online_softmax_step_t(s_t, sel, vat_ref[0, :, pl.ds(k0, kc)], m_ref, l_ref, acc_ref, A_HEADS)
        return carry

    lax.fori_loop(0, n_chunks, attend, 0)
    o_t = acc_ref[...] / l_ref[...]
    o_t = jnp.concatenate([o_t[:, h * qb:(h + 1) * qb] for h in range(A_HEADS)], axis=0)
    o_ref[0] = o_t.T.astype(o_ref.dtype)


def _dsa(qa_t, qi_t, misc_t, ki, ka, va_t):
    bsz, _, s = qa_t.shape
    qb = Q_BLOCK
    assert s % KEY_CHUNK == 0 and KEY_CHUNK >= A_TOPK_MAX
    topk = min(A_TOPK_MAX, s // 4)
    idx_bits = int(np.ceil(np.log2(s)))
    feat = lambda rows: pl.BlockSpec((1, rows, qb), lambda b, n: (b, 0, n))
    return pl.pallas_call(
        functools.partial(_dsa_kernel, topk=topk, idx_bits=idx_bits),
        grid=(bsz, s // qb),
        in_specs=[feat(A_WIDTH), feat(A_IDX_HEADS * A_IDX_DIM), feat(_MISC_ROWS),
                  pl.BlockSpec((1, s, A_IDX_DIM), lambda b, n: (b, 0, 0)),
                  pl.BlockSpec((1, s, HEAD_DIM), lambda b, n: (b, 0, 0)),
                  pl.BlockSpec((1, HEAD_DIM, s), lambda b, n: (b, 0, 0))],
        out_specs=pl.BlockSpec((1, qb, A_WIDTH), lambda b, n: (b, n, 0)),
        out_shape=jax.ShapeDtypeStruct((bsz, s, A_WIDTH), BF16),
        scratch_shapes=[pltpu.VMEM((s, qb), I32),
                        pltpu.VMEM((1, A_HEADS * qb), F32),
                        pltpu.VMEM((1, A_HEADS * qb), F32),
                        pltpu.VMEM((HEAD_DIM, A_HEADS * qb), F32)],
        compiler_params=_params(2),
        name="dsa",
    )(qa_t, qi_t, misc_t, ki, ka, va_t)


def _compress_kernel(xk_ref, xv_ref, pek_ref, pev_ref, w1k_ref, w1v_ref, w2k_ref, w2vt_ref,
                     ok_ref, ovt_ref):
    def hidden(x_ref, pe_ref, w1_ref):
        x = x_ref[0, 0]
        n = x.shape[0]
        nxt = pltpu.roll(x, n - 1, 0)
        a = (x + pe_ref[0:1, :]).astype(BF16)
        b = (nxt + pe_ref[1:2, :]).astype(BF16)
        hid = _dot(a, w1_ref[0]) + _dot(b, w1_ref[1])
        return (hid * jax.nn.sigmoid(hid)).astype(BF16)

    ok_ref[0, 0] = _dot(hidden(xk_ref, pek_ref, w1k_ref), w2k_ref[...]).astype(ok_ref.dtype)
    ovt_ref[0, 0] = _dot_nt(w2vt_ref[...], hidden(xv_ref, pev_ref, w1v_ref)).astype(ovt_ref.dtype)


def _compress(cmp_in, pe_k, w1_k, w2_k, pe_v, w1_v, w2_v):
    bsz, s, _ = cmp_in.shape
    nch = s // B_CMP_STRIDE
    cw = B_CMP_STRIDE * HEAD_DIM

    def chunks(t):
        t = t.reshape(bsz, nch, B_CMP_STRIDE, B_GROUPS, HEAD_DIM)
        return t.transpose(0, 3, 1, 2, 4).reshape(bsz, B_GROUPS, nch, cw)

    xk = chunks(cmp_in[:, :, :LANES])
    xv = chunks(cmp_in[:, :, LANES:])
    xspec = pl.BlockSpec((1, 1, nch, cw), lambda b, g: (b, g, 0, 0))
    c2 = lambda b, g: (0, 0)
    c3 = lambda b, g: (0, 0, 0)
    return pl.pallas_call(
        _compress_kernel,
        grid=(bsz, B_GROUPS),
        in_specs=[xspec, xspec,
                  pl.BlockSpec((2, cw), c2), pl.BlockSpec((2, cw), c2),
                  pl.BlockSpec((2, cw, B_CMP_HIDDEN), c3), pl.BlockSpec((2, cw, B_CMP_HIDDEN), c3),
                  pl.BlockSpec((B_CMP_HIDDEN, HEAD_DIM), c2), pl.BlockSpec((HEAD_DIM, B_CMP_HIDDEN), c2)],
        out_specs=[pl.BlockSpec((1, 1, nch, HEAD_DIM), lambda b, g: (b, g, 0, 0)),
                   pl.BlockSpec((1, 1, HEAD_DIM, nch), lambda b, g: (b, g, 0, 0))],
        out_shape=[jax.ShapeDtypeStruct((bsz, B_GROUPS, nch, HEAD_DIM), BF16),
                   jax.ShapeDtypeStruct((bsz, B_GROUPS, HEAD_DIM, nch), BF16)],
        compiler_params=_params(2),
        name="compress",
    )(xk, xv, pe_k.reshape(2, cw), pe_v.reshape(2, cw),
      w1_k.astype(BF16).reshape(2, cw, B_CMP_HIDDEN), w1_v.astype(BF16).reshape(2, cw, B_CMP_HIDDEN),
      w2_k.astype(BF16), w2_v.T.astype(BF16))


def _nsa_kernel(qbt_ref, misct_ref, kcmp_ref, vcmpt_ref, ks_ref, vst_ref, kw_ref, vwt_ref, ovlt_ref,
                o_ref, chosen_ref, m_ref, l_ref, acc_ref, out_ref, *, n_slc, top_n):
    qb = Q_BLOCK
    kc = KEY_CHUNK
    rep = B_REP
    slc_shift = int(np.log2(B_SLC_LEN))
    q0 = pl.program_id(1) * qb
    n_chunks = lax.div(q0 + qb + kc - 1, kc)
    tq = q0 + lax.broadcasted_iota(I32, (1, qb), 1)
    gates = misct_ref[0]
    n_cmp = kcmp_ref.shape[2]
    n_rows = chosen_ref.shape[0]
    cmp_end = lax.broadcasted_iota(I32, (n_cmp, qb), 0) * B_CMP_STRIDE + (B_CMP_LEN - 1)
    mask_c = cmp_end <= tq
    jj = lax.broadcasted_iota(I32, (n_rows, qb), 0)
    jj_f = jj.astype(F32)
    blk_q = lax.shift_right_logical(tq, slc_shift)
    forced = (jj == 0) | (jj == blk_q) | (jj == blk_q - 1)
    row_k = lax.broadcasted_iota(I32, (kc, qb), 0)
    w0 = pl.multiple_of(jnp.maximum(q0 - B_WINDOW, 0), qb)
    wlen = B_WINDOW + qb
    pos_w = w0 + lax.broadcasted_iota(I32, (wlen, qb), 0)
    mask_w = (pos_w <= tq) & (pos_w > tq - B_WINDOW)

    for g in range(B_GROUPS):
        q4 = _heads_on_lanes(qbt_ref[0, g * rep * HEAD_DIM:(g + 1) * rep * HEAD_DIM, :], rep, HEAD_DIM)
        vrows = slice(g * HEAD_DIM, (g + 1) * HEAD_DIM)

        s_c = _dot(kcmp_ref[0, g], q4)
        p_c = [_masked_softmax_t(s_c[:, r * qb:(r + 1) * qb], mask_c) for r in range(rep)]
        o_c = _dot(vcmpt_ref[0, g], jnp.concatenate([p.astype(BF16) for p in p_c], axis=1))

        p_sum = p_c[0]
        for r in range(1, rep):
            p_sum = p_sum + p_c[r]
        hi = p_sum.astype(BF16)
        r1 = p_sum - hi.astype(F32)
        mid = r1.astype(BF16)
        lo = (r1 - mid.astype(F32)).astype(BF16)
        ovl = ovlt_ref[...]
        imp = _dot(ovl, hi) + _dot(ovl, mid) + _dot(ovl, lo)
        val = jnp.where(jj <= blk_q, imp + FORCE_BONUS * jnp.where(forced, 1.0, 0.0), NEG)
        val = jnp.where(jj < n_slc, val, -jnp.inf)

        def pick(i, carry):
            cur, chosen = carry
            mx = jnp.max(cur, axis=0, keepdims=True)
            first = jnp.min(jnp.where(cur == mx, jj_f, float(n_rows)), axis=0, keepdims=True)
            hit = jj_f == first
            return jnp.where(hit, -jnp.inf, cur), jnp.where(hit, 1.0, chosen)

        _, chosen = lax.fori_loop(0, top_n, pick, (val, jnp.zeros((n_rows, qb), F32)))
        chosen_ref[...] = chosen

        m_ref[...] = jnp.full_like(m_ref, NEG)
        l_ref[...] = jnp.zeros_like(l_ref)
        acc_ref[...] = jnp.zeros_like(acc_ref)

        def attend(c, carry):
            k0 = pl.multiple_of(c * kc, kc)
            j0 = c * (kc // B_SLC_LEN)
            blocks = [jnp.broadcast_to(chosen_ref[pl.ds(j0 + i, 1), :], (B_SLC_LEN, qb))
                      for i in range(kc // B_SLC_LEN)]
            sel = (jnp.concatenate(blocks, axis=0) > 0.5) & (k0 + row_k <= tq)
            s_t = _dot(ks_ref[0, g, pl.ds(k0, kc), :], q4)
            _online_softmax_step_t(s_t, sel, vst_ref[0, vrows, pl.ds(k0, kc)], m_ref, l_ref, acc_ref, rep)
            return carry

        lax.fori_loop(0, n_chunks, attend, 0)
        o_s = acc_ref[...] / l_ref[...]

        s_w = _dot(kw_ref[0, g, pl.ds(w0, wlen), :], q4)
        p_w = [_masked_softmax_t(s_w[:, r * qb:(r + 1) * qb], mask_w).astype(BF16) for r in range(rep)]
        o_w = _dot(vwt_ref[0, vrows, pl.ds(w0, wlen)], jnp.concatenate(p_w, axis=1))

        for r in range(rep):
            h = g * rep + r
            sl = slice(r * qb, (r + 1) * qb)
            out_ref[h * HEAD_DIM:(h + 1) * HEAD_DIM, :] = (
                gates[h:h + 1, :] * o_c[:, sl] + gates[B_HEADS + h:B_HEADS + h + 1, :] * o_s[:, sl]
                + gates[2 * B_HEADS + h:2 * B_HEADS + h + 1, :] * o_w[:, sl])

    o_ref[0] = out_ref[...].T.astype(o_ref.dtype)


def _nsa(qb_t, misc_t, kcmp, vcmp_t, ks, vs_t, kw, vw_t):
    bsz, _, s = qb_t.shape
    qblk = Q_BLOCK
    n_cmp = kcmp.shape[2]
    n_slc = s // B_SLC_LEN
    top_n = min(B_SLC_TOP, n_slc)
    n_rows = -(-n_slc // SUBLANES) * SUBLANES
    assert s % KEY_CHUNK == 0 and s >= B_WINDOW + qblk and KEY_CHUNK % B_SLC_LEN == 0
    slc_start = np.arange(n_rows)[:, None] * B_SLC_LEN
    cmp_start = np.arange(n_cmp)[None, :] * B_CMP_STRIDE
    overlap_t = ((cmp_start < slc_start + B_SLC_LEN) & (slc_start < cmp_start + B_CMP_LEN)
                 & (np.arange(n_rows)[:, None] < n_slc))
    overlap_t = jnp.asarray(overlap_t, BF16)
    feat = lambda rows: pl.BlockSpec((1, rows, qblk), lambda b, n: (b, 0, n))
    k_spec = pl.BlockSpec((1, B_GROUPS, s, HEAD_DIM), lambda b, n: (b, 0, 0, 0))
    vt_spec = pl.BlockSpec((1, B_KV_WIDTH, s), lambda b, n: (b, 0, 0))
    return pl.pallas_call(
        functools.partial(_nsa_kernel, n_slc=n_slc, top_n=top_n),
        grid=(bsz, s // qblk),
        in_specs=[feat(B_WIDTH), feat(_MISC_ROWS),
                  pl.BlockSpec((1, B_GROUPS, n_cmp, HEAD_DIM), lambda b, n: (b, 0, 0, 0)),
                  pl.BlockSpec((1, B_GROUPS, HEAD_DIM, n_cmp), lambda b, n: (b, 0, 0, 0)),
                  k_spec, vt_spec, k_spec, vt_spec,
                  pl.BlockSpec((n_rows, n_cmp), lambda b, n: (0, 0))],
        out_specs=pl.BlockSpec((1, qblk, B_WIDTH), lambda b, n: (b, n, 0)),
        out_shape=jax.ShapeDtypeStruct((bsz, s, B_WIDTH), BF16),
        scratch_shapes=[pltpu.VMEM((n_rows, qblk), F32),
                        pltpu.VMEM((1, B_REP * qblk), F32),
                        pltpu.VMEM((1, B_REP * qblk), F32),
                        pltpu.VMEM((HEAD_DIM, B_REP * qblk), F32),
                        pltpu.VMEM((B_WIDTH, qblk), F32)],
        compiler_params=_params(2),
        name="nsa",
    )(qb_t, misc_t, kcmp, vcmp_t, ks, vs_t, kw, vw_t, overlap_t)


def _merge_kernel(x_ref, mod_ref, ya_ref, yb_ref, wgm_ref, wa_ref, wb_ref, wo_ref, g_ref, b_ref,
                  o_ref, *, alpha):
    x = x_ref[0]
    d = x.shape[-1]
    sh = mod_ref[0, 3:4, :]
    sc = mod_ref[0, 4:5, :]
    gt = mod_ref[0, 5:6, :]
    u = (x * (1.0 + sc) + sh).astype(BF16)
    gm = jax.nn.sigmoid(_dot(u, wgm_ref[...]))
    merged = gm[:, :d] * _dot(ya_ref[0], wa_ref[...]) + gm[:, d:] * _dot(yb_ref[0], wb_ref[...])
    mix = _dot(merged.astype(BF16), wo_ref[...])
    o_ref[0] = _layer_norm(alpha * x + gt * mix, g_ref[...], b_ref[...])


def _merge(x, mod, ya, yb, w_gm, w_br_a, w_br_b, w_mix_out, ln_g, ln_b, alpha):
    bsz, s, d = x.shape
    tm = TOKEN_TILE
    c2 = lambda b, i: (0, 0)
    tok = lambda width: pl.BlockSpec((1, tm, width), lambda b, i: (b, i, 0))
    return pl.pallas_call(
        functools.partial(_merge_kernel, alpha=alpha),
        grid=(bsz, s // tm),
        in_specs=[tok(d),
                  pl.BlockSpec((1, N_MOD, d), lambda b, i: (b, 0, 0)),
                  tok(A_WIDTH), tok(B_WIDTH),
                  pl.BlockSpec((d, 2 * d), c2),
                  pl.BlockSpec((A_WIDTH, d), c2),
                  pl.BlockSpec((B_WIDTH, d), c2),
                  pl.BlockSpec((d, d), c2),
                  pl.BlockSpec((1, d), c2), pl.BlockSpec((1, d), c2)],
        out_specs=tok(d),
        out_shape=jax.ShapeDtypeStruct((bsz, s, d), F32),
        compiler_params=_params(2),
        name="merge",
    )(x, mod, ya, yb, w_gm.astype(BF16), w_br_a.astype(BF16), w_br_b.astype(BF16),
      w_mix_out.astype(BF16), ln_g.reshape(1, d), ln_b.reshape(1, d))


def kernel(x, c, w_ada, b_ada, ffn1_w_in, ffn1_w_out, ln1_g, ln1_b, w_mix_in, kv_norm_g, w_kv_up,
           cmp_pe_k, cmp_w1_k, cmp_w2_k, cmp_pe_v, cmp_w1_v, cmp_w2_v, w_br_a, w_br_b, w_mix_out,
           ln2_g, ln2_b, ffn2_w_in, ffn2_w_out, ln3_g, ln3_b):
    bsz, s, d = x.shape
    depth = w_ada.shape[0]
    alpha = (2.0 * depth) ** 0.25
    assert s % TOKEN_TILE == 0 and s % KEY_CHUNK == 0
    for l in range(depth):
        mod = _ada(c, w_ada[l], b_ada[l]).reshape(bsz, N_MOD, d)
        x = _ffn(x, mod, ffn1_w_in[l], ffn1_w_out[l], ln1_g[l], ln1_b[l], 0, alpha)
        qa_t, qb_t, qi_t, va_t, vs_t, vw_t, misc_t, ka, ki, ks, kw, cmp_in = _mixin(
            x, mod, w_mix_in[l], kv_norm_g[l], w_kv_up[l])
        ya = _dsa(qa_t, qi_t, misc_t, ki, ka, va_t)
        kcmp, vcmp_t = _compress(cmp_in, cmp_pe_k[l], cmp_w1_k[l], cmp_w2_k[l],
                                 cmp_pe_v[l], cmp_w1_v[l], cmp_w2_v[l])
        yb = _nsa(qb_t, misc_t, kcmp, vcmp_t, ks, vs_t, kw, vw_t)
        w_gm = w_mix_in[l][:, -2 * d:]
        x = _merge(x, mod, ya, yb, w_gm, w_br_a[l], w_br_b[l], w_mix_out[l], ln2_g[l], ln2_b[l], alpha)
        x = _ffn(x, mod, ffn2_w_in[l], ffn2_w_out[l], ln3_g[l], ln3_b[l], 6, alpha)
    return x
```

```python
import functools

import numpy as np
import jax
import jax.numpy as jnp
from jax import lax
from jax.experimental import pallas as pl
from jax.experimental.pallas import tpu as pltpu

F32 = jnp.float32
BF16 = jnp.bfloat16
I32 = jnp.int32
I16 = jnp.int16

HEAD_DIM = 64
ROPE_THETA = 10000.0
A_HEADS = 8
A_WIDTH = A_HEADS * HEAD_DIM
A_KV_RANK = 128
A_IDX_HEADS = 4
A_IDX_DIM = 64
A_TOPK_MAX = 256
B_HEADS = 8
B_GROUPS = 2
B_REP = B_HEADS // B_GROUPS
B_WIDTH = B_HEADS * HEAD_DIM
B_KV_WIDTH = B_GROUPS * HEAD_DIM
B_CMP_STRIDE = 16
B_CMP_LEN = 2 * B_CMP_STRIDE
B_CMP_HIDDEN = 256
B_SLC_LEN = 64
B_SLC_TOP = 16
B_WINDOW = 512
N_MOD = 9
LN_EPS = 1e-5
NEG = -1e30
FORCE_BONUS = 1e4

LANES = 128
SUBLANES = 8
Q_BLOCK = LANES
KEY_CHUNK = 512
TOKEN_TILE = 512
FF_CHUNK = 256
COUNT16_ROWS = 64
ONES_ROWS = 16
VMEM_LIMIT = 56 * 1024 * 1024

NEG_KEY = int(np.array(NEG, np.float32).view(np.int32)) ^ 0x7FFFFFFF
INT_MIN = -(2 ** 31)
I16_MIN = -(2 ** 15)
INT_MAX = 2 ** 31 - 1


def _dot(a, b):
    return jnp.dot(a, b, preferred_element_type=F32)


def _dot_nt(a, b):
    return lax.dot_general(a, b, (((1,), (1,)), ((), ())), preferred_element_type=F32)


def _layer_norm(y, g, b):
    mu = jnp.mean(y, axis=-1, keepdims=True)
    d = y - mu
    var = jnp.mean(d * d, axis=-1, keepdims=True)
    return d * lax.rsqrt(var + LN_EPS) * g + b


def _params(n_axes):
    return pltpu.CompilerParams(dimension_semantics=("arbitrary",) * n_axes,
                                vmem_limit_bytes=VMEM_LIMIT)


def _ada_kernel(c_ref, w_ref, b_ref, o_ref):
    c = c_ref[...]
    ca = c * jax.nn.sigmoid(c)
    o_ref[...] = jnp.dot(ca, w_ref[...], preferred_element_type=F32,
                         precision=lax.Precision.HIGHEST) + b_ref[...]


def _ada(c, w_ada, b_ada):
    bsz, d = c.shape
    n = w_ada.shape[1]
    tn = n // 8
    return pl.pallas_call(
        _ada_kernel,
        grid=(n // tn,),
        in_specs=[pl.BlockSpec((bsz, d), lambda j: (0, 0)),
                  pl.BlockSpec((d, tn), lambda j: (0, j)),
                  pl.BlockSpec((1, tn), lambda j: (0, j))],
        out_specs=pl.BlockSpec((bsz, tn), lambda j: (0, j)),
        out_shape=jax.ShapeDtypeStruct((bsz, n), F32),
        compiler_params=_params(1),
        name="ada",
    )(c, w_ada, b_ada.reshape(1, n))


def _ffn_kernel(x_ref, mod_ref, wa_ref, wb_ref, wo_ref, g_ref, b_ref, o_ref, u_ref, acc_ref,
                *, mod_base, n_chunks, alpha):
    x = x_ref[0]
    sh = mod_ref[0, mod_base:mod_base + 1, :]
    sc = mod_ref[0, mod_base + 1:mod_base + 2, :]
    gt = mod_ref[0, mod_base + 2:mod_base + 3, :]
    u_ref[...] = (x * (1.0 + sc) + sh).astype(BF16)
    acc_ref[...] = jnp.zeros_like(acc_ref)

    def body(j, carry):
        u = u_ref[...]
        a = _dot(u, wa_ref[j])
        b = _dot(u, wb_ref[j])
        h = (a * jax.nn.sigmoid(a) * b).astype(BF16)
        acc_ref[...] += _dot(h, wo_ref[j])
        return carry

    lax.fori_loop(0, n_chunks, body, 0)
    y = alpha * x + (0.5 * gt) * acc_ref[...]
    o_ref[0] = _layer_norm(y, g_ref[...], b_ref[...])


def _ffn(x, mod, w_in, w_out, ln_g, ln_b, mod_base, alpha):
    bsz, s, d = x.shape
    d_ff = w_out.shape[0]
    n_chunks = d_ff // FF_CHUNK
    tm = TOKEN_TILE
    wa = w_in[:, :d_ff].astype(BF16).reshape(d, n_chunks, FF_CHUNK).transpose(1, 0, 2)
    wb = w_in[:, d_ff:].astype(BF16).reshape(d, n_chunks, FF_CHUNK).transpose(1, 0, 2)
    wo = w_out.astype(BF16).reshape(n_chunks, FF_CHUNK, d)
    const3 = lambda b, i: (0, 0, 0)
    return pl.pallas_call(
        functools.partial(_ffn_kernel, mod_base=mod_base, n_chunks=n_chunks, alpha=alpha),
        grid=(bsz, s // tm),
        in_specs=[pl.BlockSpec((1, tm, d), lambda b, i: (b, i, 0)),
                  pl.BlockSpec((1, N_MOD, d), lambda b, i: (b, 0, 0)),
                  pl.BlockSpec((n_chunks, d, FF_CHUNK), const3),
                  pl.BlockSpec((n_chunks, d, FF_CHUNK), const3),
                  pl.BlockSpec((n_chunks, FF_CHUNK, d), const3),
                  pl.BlockSpec((1, d), lambda b, i: (0, 0)),
                  pl.BlockSpec((1, d), lambda b, i: (0, 0))],
        out_specs=pl.BlockSpec((1, tm, d), lambda b, i: (b, i, 0)),
        out_shape=jax.ShapeDtypeStruct((bsz, s, d), F32),
        scratch_shapes=[pltpu.VMEM((tm, d), BF16), pltpu.VMEM((tm, d), F32)],
        compiler_params=_params(2),
        name="ffn",
    )(x, mod, wa, wb, wo, ln_g.reshape(1, d), ln_b.reshape(1, d))


_S_KC = 0
_S_VC = _S_KC + B_KV_WIDTH
_S_KS = _S_VC + B_KV_WIDTH
_S_KW = _S_KS + B_KV_WIDTH
_S_LAT = _S_KW + B_KV_WIDTH
_S_KI = _S_LAT + A_KV_RANK
_P_STD = _S_KI + LANES
_T_QA = 0
_T_QB = _T_QA + A_WIDTH
_T_QI = _T_QB + B_WIDTH
_T_VS = _T_QI + A_IDX_HEADS * A_IDX_DIM
_T_VW = _T_VS + B_KV_WIDTH
_T_MISC = _T_VW + B_KV_WIDTH
_MISC_ROWS = 32
_MISC_WI = 3 * B_HEADS
_P_T = _T_MISC + _MISC_ROWS


def _mixin_kernel(x_ref, mod_ref, ws_ref, wt_ref, cos_ref, sin_ref, cost_ref, sint_ref,
                  kvg_ref, wk_ref, wvt_ref,
                  qat_ref, qbt_ref, qit_ref, vat_ref, vst_ref, vwt_ref, misct_ref,
                  ka_ref, ki_ref, ks_ref, kw_ref, cmp_ref, hs_ref, ht_ref):
    x = x_ref[0]
    sh = mod_ref[0, 3:4, :]
    sc = mod_ref[0, 4:5, :]
    u = (x * (1.0 + sc) + sh).astype(BF16)
    hs_ref[...] = _dot(u, ws_ref[...])
    ht_ref[...] = _dot_nt(wt_ref[...], u)

    tm = x.shape[0]
    half = HEAD_DIM // 2
    scale = HEAD_DIM ** -0.5 * float(np.log2(np.e))

    cos_t = cost_ref[...]
    sin_t = sint_ref[...]

    def rope_t(src0, dst_ref, dst0, mul):
        x1 = ht_ref[src0:src0 + half, :]
        x2 = ht_ref[src0 + half:src0 + HEAD_DIM, :]
        o1 = x1 * cos_t - x2 * sin_t
        o2 = x1 * sin_t + x2 * cos_t
        if mul != 1.0:
            o1 = o1 * mul
            o2 = o2 * mul
        dst_ref[0, dst0:dst0 + half, :] = o1.astype(dst_ref.dtype)
        dst_ref[0, dst0 + half:dst0 + HEAD_DIM, :] = o2.astype(dst_ref.dtype)

    for h in range(A_HEADS):
        rope_t(_T_QA + h * HEAD_DIM, qat_ref, h * HEAD_DIM, scale)
    for h in range(B_HEADS):
        rope_t(_T_QB + h * HEAD_DIM, qbt_ref, h * HEAD_DIM, scale)
    for h in range(A_IDX_HEADS):
        rope_t(_T_QI + h * A_IDX_DIM, qit_ref, h * A_IDX_DIM, 1.0)
    vst_ref[0] = ht_ref[_T_VS:_T_VS + B_KV_WIDTH, :].astype(BF16)
    vwt_ref[0] = ht_ref[_T_VW:_T_VW + B_KV_WIDTH, :].astype(BF16)
    misc = ht_ref[_T_MISC:_T_MISC + _MISC_ROWS, :]
    row = lax.broadcasted_iota(I32, misc.shape, 0)
    misct_ref[0] = jnp.where(row < _MISC_WI, jax.nn.sigmoid(misc), misc * (A_IDX_HEADS ** -0.5))

    cos = cos_ref[...]
    sin = sin_ref[...]
    lane = lax.broadcasted_iota(I32, (tm, LANES), 1)
    first_half = (lane & (HEAD_DIM - 1)) < half

    def rope(t):
        rot = jnp.where(first_half, pltpu.roll(t, LANES - half, 1), pltpu.roll(t, half, 1))
        return t * cos + rot * sin

    def chunk(c0):
        return hs_ref[:, c0:c0 + LANES]

    cmp_ref[0, :, :LANES] = rope(chunk(_S_KC))
    cmp_ref[0, :, LANES:] = chunk(_S_VC)
    ks = rope(chunk(_S_KS)).astype(BF16)
    kw = rope(chunk(_S_KW)).astype(BF16)
    for g in range(B_GROUPS):
        ks_ref[0, g] = ks[:, g * HEAD_DIM:(g + 1) * HEAD_DIM]
        kw_ref[0, g] = kw[:, g * HEAD_DIM:(g + 1) * HEAD_DIM]
    ki_ref[0] = rope(chunk(_S_KI))[:, :A_IDX_DIM].astype(BF16)

    lat = chunk(_S_LAT)
    latn = (lat * lax.rsqrt(jnp.mean(lat * lat, axis=-1, keepdims=True) + LN_EPS)
            * kvg_ref[...]).astype(BF16)
    ka_ref[0] = rope(_dot(latn, wk_ref[...]))[:, :HEAD_DIM].astype(BF16)
    vat_ref[0] = _dot_nt(wvt_ref[...], latn).astype(BF16)


def _rope_tables(s):
    half = HEAD_DIM // 2
    inv = ROPE_THETA ** (-jnp.arange(half, dtype=F32) / half)
    ang = jnp.arange(s).astype(F32)[:, None] * inv[None, :]
    cos = jnp.cos(ang)
    sin = jnp.sin(ang)
    reps = LANES // HEAD_DIM
    cos_l = jnp.tile(jnp.concatenate([cos, cos], axis=-1), (1, reps))
    sin_l = jnp.tile(jnp.concatenate([-sin, sin], axis=-1), (1, reps))
    return cos_l, sin_l, cos.T, sin.T


def _mixin(x, mod, w_mix_in, kv_norm_g, w_kv_up):
    bsz, s, d = x.shape
    tm = TOKEN_TILE
    split = np.cumsum([A_WIDTH, A_KV_RANK, A_IDX_HEADS * A_IDX_DIM, A_IDX_DIM, A_IDX_HEADS,
                       B_WIDTH, 6 * B_KV_WIDTH, 3 * B_HEADS]).tolist()
    w_qa, w_lat, w_qi, w_ki, w_wi, w_qb, w_kv6, w_gb, _ = jnp.split(w_mix_in, split, axis=-1)
    w_kc, w_vc, w_ks, w_vs, w_kw, w_vw = jnp.split(w_kv6, 6, axis=-1)
    zeros = lambda n: jnp.zeros((d, n), w_mix_in.dtype)
    w_std = jnp.concatenate([w_kc, w_vc, w_ks, w_kw, w_lat, w_ki, zeros(LANES - A_IDX_DIM)],
                            axis=-1).astype(BF16)
    w_t = jnp.concatenate([w_qa, w_qb, w_qi, w_vs, w_vw, w_gb, w_wi,
                           zeros(_MISC_ROWS - 3 * B_HEADS - A_IDX_HEADS)], axis=-1).T.astype(BF16)
    assert w_std.shape[1] == _P_STD and w_t.shape[0] == _P_T
    cos_l, sin_l, cos_t, sin_t = _rope_tables(s)
    w_k = jnp.concatenate([w_kv_up[:, :HEAD_DIM], jnp.zeros((A_KV_RANK, LANES - HEAD_DIM), F32)],
                          axis=-1).astype(BF16)
    w_vt = w_kv_up[:, HEAD_DIM:].T.astype(BF16)
    c2 = lambda b, i: (0, 0)
    feat = lambda rows: pl.BlockSpec((1, rows, tm), lambda b, i: (b, 0, i))
    tok = lambda width: pl.BlockSpec((1, tm, width), lambda b, i: (b, i, 0))
    grp = pl.BlockSpec((1, B_GROUPS, tm, HEAD_DIM), lambda b, i: (b, 0, i, 0))
    sds = jax.ShapeDtypeStruct
    out_shape = [sds((bsz, A_WIDTH, s), BF16),
                 sds((bsz, B_WIDTH, s), BF16),
                 sds((bsz, A_IDX_HEADS * A_IDX_DIM, s), BF16),
                 sds((bsz, HEAD_DIM, s), BF16),
                 sds((bsz, B_KV_WIDTH, s), BF16),
                 sds((bsz, B_KV_WIDTH, s), BF16),
                 sds((bsz, _MISC_ROWS, s), F32),
                 sds((bsz, s, HEAD_DIM), BF16),
                 sds((bsz, s, A_IDX_DIM), BF16),
                 sds((bsz, B_GROUPS, s, HEAD_DIM), BF16),
                 sds((bsz, B_GROUPS, s, HEAD_DIM), BF16),
                 sds((bsz, s, 2 * LANES), F32)]
    out_specs = [feat(A_WIDTH), feat(B_WIDTH), feat(A_IDX_HEADS * A_IDX_DIM), feat(HEAD_DIM),
                 feat(B_KV_WIDTH), feat(B_KV_WIDTH), feat(_MISC_ROWS),
                 tok(HEAD_DIM), tok(A_IDX_DIM), grp, grp, tok(2 * LANES)]
    return pl.pallas_call(
        _mixin_kernel,
        grid=(bsz, s // tm),
        in_specs=[pl.BlockSpec((1, tm, d), lambda b, i: (b, i, 0)),
                  pl.BlockSpec((1, N_MOD, d), lambda b, i: (b, 0, 0)),
                  pl.BlockSpec((d, _P_STD), c2),
                  pl.BlockSpec((_P_T, d), c2),
                  pl.BlockSpec((tm, LANES), lambda b, i: (i, 0)),
                  pl.BlockSpec((tm, LANES), lambda b, i: (i, 0)),
                  pl.BlockSpec((HEAD_DIM // 2, tm), lambda b, i: (0, i)),
                  pl.BlockSpec((HEAD_DIM // 2, tm), lambda b, i: (0, i)),
                  pl.BlockSpec((1, A_KV_RANK), c2),
                  pl.BlockSpec((A_KV_RANK, LANES), c2),
                  pl.BlockSpec((HEAD_DIM, A_KV_RANK), c2)],
        out_specs=out_specs,
        out_shape=out_shape,
        scratch_shapes=[pltpu.VMEM((tm, _P_STD), F32), pltpu.VMEM((_P_T, tm), F32)],
        compiler_params=_params(2),
        name="mixin",
    )(x, mod, w_std, w_t, cos_l, sin_l, cos_t, sin_t,
      kv_norm_g.reshape(1, A_KV_RANK), w_k, w_vt)


def _heads_on_lanes(qt, n_heads, dim):
    return jnp.concatenate([qt[h * dim:(h + 1) * dim, :] for h in range(n_heads)], axis=1)


def _fold_rows(x, op, tile=SUBLANES):
    while x.shape[0] % (2 * tile) == 0:
        h = x.shape[0] // 2
        x = op(x[:h], x[h:])
    return x


def _max_rows(x):
    return jnp.max(_fold_rows(x, jnp.maximum), axis=0, keepdims=True)


def _sum_rows(x):
    return jnp.sum(_fold_rows(x, jnp.add), axis=0, keepdims=True)


def _mask_cap(sel):
    return jnp.where(sel, jnp.inf, NEG)


def _with_ones_rows(v_t):
    return jnp.concatenate([v_t, jnp.ones((ONES_ROWS, v_t.shape[1]), v_t.dtype)], axis=0)


def _online_softmax_step_t(s_t, cap, v_ext, m_ref, acc_ref, n_heads):
    q = cap.shape[1]
    ps, alphas = [], []
    for h in range(n_heads):
        sl = slice(h * q, (h + 1) * q)
        sm = jnp.minimum(s_t[:, sl], cap)
        m_old = m_ref[:, sl]
        m_new = jnp.maximum(m_old, _max_rows(sm))
        ps.append(jnp.exp2(sm - m_new).astype(BF16))
        alphas.append(jnp.exp2(m_old - m_new))
        m_ref[:, sl] = m_new
    acc_ref[...] = (jnp.concatenate(alphas, axis=1) * acc_ref[...]
                    + _dot(v_ext, jnp.concatenate(ps, axis=1)))


def _pipelined_attention_t(n_chunks, scores_fn, cap_fn, v_fn, sa_ref, sb_ref, m_ref, acc_ref, n_heads):
    last = 2 * n_chunks - 1
    sa_ref[...] = scores_fn(0)

    def body(c, carry):
        i0 = 2 * c
        sb_ref[...] = scores_fn(i0 + 1)
        _online_softmax_step_t(sa_ref[...], cap_fn(i0), _with_ones_rows(v_fn(i0)), m_ref, acc_ref, n_heads)
        sa_ref[...] = scores_fn(jnp.minimum(i0 + 2, last))
        _online_softmax_step_t(sb_ref[...], cap_fn(i0 + 1), _with_ones_rows(v_fn(i0 + 1)),
                               m_ref, acc_ref, n_heads)
        return carry

    lax.fori_loop(0, n_chunks, body, 0)


def _finish_online_softmax_t(m_ref, acc_ref):
    d = acc_ref.shape[0] - ONES_ROWS
    return acc_ref[:d, :] * (jnp.where(m_ref[...] > NEG, 1.0, 0.0) / acc_ref[d:d + 1, :])


def _masked_softmax_t(s, cap):
    sm = jnp.minimum(s, cap)
    mx = _max_rows(sm)
    e = jnp.exp2(sm - mx)
    return e * (jnp.where(mx > NEG, 1.0, 0.0) / _sum_rows(e))


def _dsa_kernel(qat_ref, qit_ref, misct_ref, ki_ref, ka_ref, vat_ref, o_ref,
                key_ref, hi_ref, lo_ref, sa_ref, sb_ref, m_ref, acc_ref, *, topk, idx_bits):
    qb = Q_BLOCK
    kc = KEY_CHUNK
    q0 = pl.program_id(1) * qb
    n_chunks = lax.div(q0 + qb + kc - 1, kc)
    tq = q0 + lax.broadcasted_iota(I32, (1, qb), 1)
    row_k = lax.broadcasted_iota(I32, (kc, qb), 0)
    wi = misct_ref[0, _MISC_WI:_MISC_WI + A_IDX_HEADS, :]
    qi_t = qit_ref[0]
    qi_pairs = [_heads_on_lanes(qi_t[p * 2 * A_IDX_DIM:(p + 1) * 2 * A_IDX_DIM], 2, A_IDX_DIM)
                for p in range(A_IDX_HEADS // 2)]

    def score_chunk(c, carry):
        k0 = pl.multiple_of(c * kc, kc)
        ki = ki_ref[0, pl.ds(k0, kc), :]
        acc = jnp.zeros((kc, qb), F32)
        for p in range(A_IDX_HEADS // 2):
            lg = _dot(ki, qi_pairs[p])
            for j in range(2):
                h = 2 * p + j
                acc = acc + wi[h:h + 1, :] * jnp.maximum(lg[:, j * qb:(j + 1) * qb], 0.0)
        sc = jnp.where(k0 + row_k <= tq, acc, NEG)
        sc = jnp.where(sc == 0.0, 0.0, sc)
        bits = pltpu.bitcast(sc, I32)
        key = jnp.where(bits < 0, bits ^ 0x7FFFFFFF, bits)
        key_ref[pl.ds(k0, kc), :] = key
        hi = lax.shift_right_arithmetic(key, 16)
        hi_ref[pl.ds(k0, kc), :] = hi.astype(I16)
        lo_ref[pl.ds(k0, kc), :] = ((key & 0xFFFF) + I16_MIN).astype(I16)
        return carry

    lax.fori_loop(0, n_chunks, score_chunk, 0)

    def count(pred):
        def body(c, acc):
            k0 = pl.multiple_of(c * kc, kc)
            hit = jnp.where(pred(key_ref[pl.ds(k0, kc), :], k0 + row_k), 1.0, 0.0)
            return acc + _fold_rows(hit, jnp.add)
        acc = lax.fori_loop(0, n_chunks, body, jnp.zeros((SUBLANES, qb), F32))
        return jnp.sum(acc, axis=0, keepdims=True)

    wide = 2 * kc
    n_wide = lax.div(q0 + qb + wide - 1, wide)

    @pl.when(n_wide * 2 > n_chunks)
    def _():
        hi_ref[pl.ds(pl.multiple_of(n_chunks * kc, kc), kc), :] = jnp.full((kc, qb), I16_MIN, I16)

    def count16(ref, pred):
        def body(c, acc):
            k0 = pl.multiple_of(c * wide, wide)
            hit = jnp.where(pred(ref[pl.ds(k0, wide), :]), jnp.int16(1), jnp.int16(0))
            return acc + _fold_rows(hit, jnp.add, COUNT16_ROWS)
        acc = lax.fori_loop(0, n_wide, body, jnp.zeros((COUNT16_ROWS, qb), I16))
        return jnp.sum(acc.astype(F32), axis=0, keepdims=True)

    def search16(ref, need, cnt_all):
        def step(it, carry):
            thr, cnt_thr = carry
            cand = thr + lax.shift_left(jnp.int32(1), 15 - it)
            cand16 = cand.astype(I16)
            cnt = count16(ref, lambda v: v >= cand16)
            ok = cnt >= need
            return jnp.where(ok, cand, thr), jnp.where(ok, cnt, cnt_thr)
        return lax.fori_loop(0, 16, step, (jnp.full((1, qb), I16_MIN, I32), cnt_all))

    n_keys = jnp.full((1, qb), 1.0, F32) * (n_wide * wide).astype(F32)
    thr_hi, cnt_hi = search16(hi_ref, float(topk), n_keys)
    thr_hi16 = thr_hi.astype(I16)
    cnt_above = count16(hi_ref, lambda v: v > thr_hi16)

    def park_chunk(c, carry):
        sl = pl.ds(pl.multiple_of(c * wide, wide), wide)
        lo_ref[sl, :] = jnp.where(hi_ref[sl, :] == thr_hi16, lo_ref[sl, :], jnp.int16(I16_MIN))
        return carry

    lax.fori_loop(0, n_wide, park_chunk, 0)
    thr_lo, cnt_lo = search16(lo_ref, topk - cnt_above, cnt_hi - cnt_above)
    thr = lax.shift_left(thr_hi, 16) + (thr_lo - I16_MIN)
    cnt_thr = cnt_above + cnt_lo

    has_tie = jnp.max(jnp.where((cnt_thr > topk) & (thr > NEG_KEY), 1.0, 0.0)) > 0.0

    def tie_search():
        need = topk - count(lambda key, pos: key > thr)

        def step(it, last):
            cand = last + lax.shift_left(jnp.int32(1), idx_bits - 1 - it)
            below = count(lambda key, pos: (key == thr) & (pos < cand))
            return jnp.where(below < need, cand, last)

        return lax.fori_loop(0, idx_bits, step, jnp.zeros((1, qb), I32))

    last_eq = lax.cond(has_tie, tie_search, lambda: jnp.full((1, qb), INT_MAX, I32))

    m_ref[...] = jnp.full_like(m_ref, NEG)
    acc_ref[...] = jnp.zeros_like(acc_ref)
    q_all = _heads_on_lanes(qat_ref[0], A_HEADS, HEAD_DIM)

    half = kc // 2
    row_h = lax.broadcasted_iota(I32, (half, qb), 0)

    def keys_at(i):
        return pl.ds(pl.multiple_of(i * half, half), half)

    def cap_fn(i):
        key = key_ref[keys_at(i), :]
        pos = i * half + row_h
        return _mask_cap(((key > thr) | ((key == thr) & (pos <= last_eq))) & (pos <= tq))

    _pipelined_attention_t(n_chunks,
                           lambda i: _dot(ka_ref[0, keys_at(i), :], q_all),
                           cap_fn,
                           lambda i: vat_ref[0, :, keys_at(i)],
                           sa_ref, sb_ref, m_ref, acc_ref, A_HEADS)
    o_t = _finish_online_softmax_t(m_ref, acc_ref)
    o_t = jnp.concatenate([o_t[:, h * qb:(h + 1) * qb] for h in range(A_HEADS)], axis=0)
    o_ref[0] = o_t.T.astype(o_ref.dtype)


def _dsa(qa_t, qi_t, misc_t, ki, ka, va_t):
    bsz, _, s = qa_t.shape
    qb = Q_BLOCK
    assert s % (2 * KEY_CHUNK) == 0 and KEY_CHUNK >= A_TOPK_MAX
    topk = min(A_TOPK_MAX, s // 4)
    idx_bits = int(np.ceil(np.log2(s)))
    feat = lambda rows: pl.BlockSpec((1, rows, qb), lambda b, n: (b, 0, n))
    return pl.pallas_call(
        functools.partial(_dsa_kernel, topk=topk, idx_bits=idx_bits),
        grid=(bsz, s // qb),
        in_specs=[feat(A_WIDTH), feat(A_IDX_HEADS * A_IDX_DIM), feat(_MISC_ROWS),
                  pl.BlockSpec((1, s, A_IDX_DIM), lambda b, n: (b, 0, 0)),
                  pl.BlockSpec((1, s, HEAD_DIM), lambda b, n: (b, 0, 0)),
                  pl.BlockSpec((1, HEAD_DIM, s), lambda b, n: (b, 0, 0))],
        out_specs=pl.BlockSpec((1, qb, A_WIDTH), lambda b, n: (b, n, 0)),
        out_shape=jax.ShapeDtypeStruct((bsz, s, A_WIDTH), BF16),
        scratch_shapes=[pltpu.VMEM((s, qb), I32),
                        pltpu.VMEM((s, qb), I16),
                        pltpu.VMEM((s, qb), I16),
                        pltpu.VMEM((KEY_CHUNK // 2, A_HEADS * qb), F32),
                        pltpu.VMEM((KEY_CHUNK // 2, A_HEADS * qb), F32),
                        pltpu.VMEM((1, A_HEADS * qb), F32),
                        pltpu.VMEM((HEAD_DIM + ONES_ROWS, A_HEADS * qb), F32)],
        compiler_params=_params(2),
        name="dsa",
    )(qa_t, qi_t, misc_t, ki, ka, va_t)


def _compress_kernel(xk_ref, xv_ref, pek_ref, pev_ref, w1k_ref, w1v_ref, w2k_ref, w2vt_ref,
                     ok_ref, ovt_ref):
    def hidden(x_ref, pe_ref, w1_ref):
        x = x_ref[0, 0]
        n = x.shape[0]
        nxt = pltpu.roll(x, n - 1, 0)
        a = (x + pe_ref[0:1, :]).astype(BF16)
        b = (nxt + pe_ref[1:2, :]).astype(BF16)
        hid = _dot(a, w1_ref[0]) + _dot(b, w1_ref[1])
        return (hid * jax.nn.sigmoid(hid)).astype(BF16)

    ok_ref[0, 0] = _dot(hidden(xk_ref, pek_ref, w1k_ref), w2k_ref[...]).astype(ok_ref.dtype)
    ovt_ref[0, 0] = _dot_nt(w2vt_ref[...], hidden(xv_ref, pev_ref, w1v_ref)).astype(ovt_ref.dtype)


def _compress(cmp_in, pe_k, w1_k, w2_k, pe_v, w1_v, w2_v):
    bsz, s, _ = cmp_in.shape
    nch = s // B_CMP_STRIDE
    cw = B_CMP_STRIDE * HEAD_DIM

    def chunks(t):
        t = t.reshape(bsz, nch, B_CMP_STRIDE, B_GROUPS, HEAD_DIM)
        return t.transpose(0, 3, 1, 2, 4).reshape(bsz, B_GROUPS, nch, cw)

    xk = chunks(cmp_in[:, :, :LANES])
    xv = chunks(cmp_in[:, :, LANES:])
    xspec = pl.BlockSpec((1, 1, nch, cw), lambda b, g: (b, g, 0, 0))
    c2 = lambda b, g: (0, 0)
    c3 = lambda b, g: (0, 0, 0)
    return pl.pallas_call(
        _compress_kernel,
        grid=(bsz, B_GROUPS),
        in_specs=[xspec, xspec,
                  pl.BlockSpec((2, cw), c2), pl.BlockSpec((2, cw), c2),
                  pl.BlockSpec((2, cw, B_CMP_HIDDEN), c3), pl.BlockSpec((2, cw, B_CMP_HIDDEN), c3),
                  pl.BlockSpec((B_CMP_HIDDEN, HEAD_DIM), c2), pl.BlockSpec((HEAD_DIM, B_CMP_HIDDEN), c2)],
        out_specs=[pl.BlockSpec((1, 1, nch, HEAD_DIM), lambda b, g: (b, g, 0, 0)),
                   pl.BlockSpec((1, 1, HEAD_DIM, nch), lambda b, g: (b, g, 0, 0))],
        out_shape=[jax.ShapeDtypeStruct((bsz, B_GROUPS, nch, HEAD_DIM), BF16),
                   jax.ShapeDtypeStruct((bsz, B_GROUPS, HEAD_DIM, nch), BF16)],
        compiler_params=_params(2),
        name="compress",
    )(xk, xv, pe_k.reshape(2, cw), pe_v.reshape(2, cw),
      w1_k.astype(BF16).reshape(2, cw, B_CMP_HIDDEN), w1_v.astype(BF16).reshape(2, cw, B_CMP_HIDDEN),
      w2_k.astype(BF16), w2_v.T.astype(BF16))


def _nsa_kernel(qbt_ref, misct_ref, kcmp_ref, vcmpt_ref, ks_ref, vst_ref, kw_ref, vwt_ref, ovlt_ref,
                o_ref, chosen_ref, sa_ref, sb_ref, m_ref, acc_ref, out_ref, *, n_slc, top_n):
    qb = Q_BLOCK
    kc = KEY_CHUNK
    rep = B_REP
    slc_shift = int(np.log2(B_SLC_LEN))
    q0 = pl.program_id(1) * qb
    n_chunks = lax.div(q0 + qb + kc - 1, kc)
    tq = q0 + lax.broadcasted_iota(I32, (1, qb), 1)
    gates = misct_ref[0]
    n_cmp = kcmp_ref.shape[2]
    n_rows = chosen_ref.shape[0]
    cmp_end = lax.broadcasted_iota(I32, (n_cmp, qb), 0) * B_CMP_STRIDE + (B_CMP_LEN - 1)
    cap_c = _mask_cap(cmp_end <= tq)
    jj = lax.broadcasted_iota(I32, (n_rows, qb), 0)
    jj_f = jj.astype(F32)
    blk_q = lax.shift_right_logical(tq, slc_shift)
    forced = (jj == 0) | (jj == blk_q) | (jj == blk_q - 1)
    half = kc // 2
    row_h = lax.broadcasted_iota(I32, (half, qb), 0)
    w0 = pl.multiple_of(jnp.maximum(q0 - B_WINDOW, 0), qb)
    wlen = B_WINDOW + qb
    pos_w = w0 + lax.broadcasted_iota(I32, (wlen, qb), 0)
    cap_w = _mask_cap((pos_w <= tq) & (pos_w > tq - B_WINDOW))

    for g in range(B_GROUPS):
        q4 = _heads_on_lanes(qbt_ref[0, g * rep * HEAD_DIM:(g + 1) * rep * HEAD_DIM, :], rep, HEAD_DIM)
        vrows = slice(g * HEAD_DIM, (g + 1) * HEAD_DIM)

        s_c = _dot(kcmp_ref[0, g], q4)
        p_c = [_masked_softmax_t(s_c[:, r * qb:(r + 1) * qb], cap_c) for r in range(rep)]
        o_c = _dot(vcmpt_ref[0, g], jnp.concatenate([p.astype(BF16) for p in p_c], axis=1))

        p_sum = p_c[0]
        for r in range(1, rep):
            p_sum = p_sum + p_c[r]
        hi = p_sum.astype(BF16)
        r1 = p_sum - hi.astype(F32)
        mid = r1.astype(BF16)
        lo = (r1 - mid.astype(F32)).astype(BF16)
        ovl = ovlt_ref[...]
        imp = _dot(ovl, hi) + _dot(ovl, mid) + _dot(ovl, lo)
        val = jnp.where(jj <= blk_q, imp + FORCE_BONUS * jnp.where(forced, 1.0, 0.0), NEG)
        val = jnp.where(jj < n_slc, val, -jnp.inf)

        def pick(i, carry):
            cur, chosen = carry
            mx = jnp.max(cur, axis=0, keepdims=True)
            first = jnp.min(jnp.where(cur == mx, jj_f, float(n_rows)), axis=0, keepdims=True)
            hit = jj_f == first
            return jnp.where(hit, -jnp.inf, cur), jnp.where(hit, 1.0, chosen)

        _, chosen = lax.fori_loop(0, top_n, pick, (val, jnp.zeros((n_rows, qb), F32)))
        chosen_ref[...] = chosen

        m_ref[...] = jnp.full_like(m_ref, NEG)
        acc_ref[...] = jnp.zeros_like(acc_ref)

        def keys_at(i):
            return pl.ds(pl.multiple_of(i * half, half), half)

        def cap_fn(i):
            j0 = i * (half // B_SLC_LEN)
            blocks = [jnp.broadcast_to(chosen_ref[pl.ds(j0 + b, 1), :], (B_SLC_LEN, qb))
                      for b in range(half // B_SLC_LEN)]
            return _mask_cap((jnp.concatenate(blocks, axis=0) > 0.5) & (i * half + row_h <= tq))

        _pipelined_attention_t(n_chunks,
                               lambda i: _dot(ks_ref[0, g, keys_at(i), :], q4),
                               cap_fn,
                               lambda i: vst_ref[0, vrows, keys_at(i)],
                               sa_ref, sb_ref, m_ref, acc_ref, rep)
        o_s = _finish_online_softmax_t(m_ref, acc_ref)

        s_w = _dot(kw_ref[0, g, pl.ds(w0, wlen), :], q4)
        p_w = [_masked_softmax_t(s_w[:, r * qb:(r + 1) * qb], cap_w).astype(BF16) for r in range(rep)]
        o_w = _dot(vwt_ref[0, vrows, pl.ds(w0, wlen)], jnp.concatenate(p_w, axis=1))

        for r in range(rep):
            h = g * rep + r
            sl = slice(r * qb, (r + 1) * qb)
            out_ref[h * HEAD_DIM:(h + 1) * HEAD_DIM, :] = (
                gates[h:h + 1, :] * o_c[:, sl] + gates[B_HEADS + h:B_HEADS + h + 1, :] * o_s[:, sl]
                + gates[2 * B_HEADS + h:2 * B_HEADS + h + 1, :] * o_w[:, sl])

    o_ref[0] = out_ref[...].T.astype(o_ref.dtype)


def _nsa(qb_t, misc_t, kcmp, vcmp_t, ks, vs_t, kw, vw_t):
    bsz, _, s = qb_t.shape
    qblk = Q_BLOCK
    n_cmp = kcmp.shape[2]
    n_slc = s // B_SLC_LEN
    top_n = min(B_SLC_TOP, n_slc)
    n_rows = -(-n_slc // SUBLANES) * SUBLANES
    assert s % KEY_CHUNK == 0 and s >= B_WINDOW + qblk and KEY_CHUNK % B_SLC_LEN == 0
    slc_start = np.arange(n_rows)[:, None] * B_SLC_LEN
    cmp_start = np.arange(n_cmp)[None, :] * B_CMP_STRIDE
    overlap_t = ((cmp_start < slc_start + B_SLC_LEN) & (slc_start < cmp_start + B_CMP_LEN)
                 & (np.arange(n_rows)[:, None] < n_slc))
    overlap_t = jnp.asarray(overlap_t, BF16)
    feat = lambda rows: pl.BlockSpec((1, rows, qblk), lambda b, n: (b, 0, n))
    k_spec = pl.BlockSpec((1, B_GROUPS, s, HEAD_DIM), lambda b, n: (b, 0, 0, 0))
    vt_spec = pl.BlockSpec((1, B_KV_WIDTH, s), lambda b, n: (b, 0, 0))
    return pl.pallas_call(
        functools.partial(_nsa_kernel, n_slc=n_slc, top_n=top_n),
        grid=(bsz, s // qblk),
        in_specs=[feat(B_WIDTH), feat(_MISC_ROWS),
                  pl.BlockSpec((1, B_GROUPS, n_cmp, HEAD_DIM), lambda b, n: (b, 0, 0, 0)),
                  pl.BlockSpec((1, B_GROUPS, HEAD_DIM, n_cmp), lambda b, n: (b, 0, 0, 0)),
                  k_spec, vt_spec, k_spec, vt_spec,
                  pl.BlockSpec((n_rows, n_cmp), lambda b, n: (0, 0))],
        out_specs=pl.BlockSpec((1, qblk, B_WIDTH), lambda b, n: (b, n, 0)),
        out_shape=jax.ShapeDtypeStruct((bsz, s, B_WIDTH), BF16),
        scratch_shapes=[pltpu.VMEM((n_rows, qblk), F32),
                        pltpu.VMEM((KEY_CHUNK // 2, B_REP * qblk), F32),
                        pltpu.VMEM((KEY_CHUNK // 2, B_REP * qblk), F32),
                        pltpu.VMEM((1, B_REP * qblk), F32),
                        pltpu.VMEM((HEAD_DIM + ONES_ROWS, B_REP * qblk), F32),
                        pltpu.VMEM((B_WIDTH, qblk), F32)],
        compiler_params=_params(2),
        name="nsa",
    )(qb_t, misc_t, kcmp, vcmp_t, ks, vs_t, kw, vw_t, overlap_t)


def _merge_kernel(x_ref, mod_ref, ya_ref, yb_ref, wgm_ref, wa_ref, wb_ref, wo_ref, g_ref, b_ref,
                  o_ref, *, alpha):
    x = x_ref[0]
    d = x.shape[-1]
    sh = mod_ref[0, 3:4, :]
    sc = mod_ref[0, 4:5, :]
    gt = mod_ref[0, 5:6, :]
    u = (x * (1.0 + sc) + sh).astype(BF16)
    gm = jax.nn.sigmoid(_dot(u, wgm_ref[...]))
    merged = gm[:, :d] * _dot(ya_ref[0], wa_ref[...]) + gm[:, d:] * _dot(yb_ref[0], wb_ref[...])
    mix = _dot(merged.astype(BF16), wo_ref[...])
    o_ref[0] = _layer_norm(alpha * x + gt * mix, g_ref[...], b_ref[...])


def _merge(x, mod, ya, yb, w_gm, w_br_a, w_br_b, w_mix_out, ln_g, ln_b, alpha):
    bsz, s, d = x.shape
    tm = TOKEN_TILE
    c2 = lambda b, i: (0, 0)
    tok = lambda width: pl.BlockSpec((1, tm, width), lambda b, i: (b, i, 0))
    return pl.pallas_call(
        functools.partial(_merge_kernel, alpha=alpha),
        grid=(bsz, s // tm),
        in_specs=[tok(d),
                  pl.BlockSpec((1, N_MOD, d), lambda b, i: (b, 0, 0)),
                  tok(A_WIDTH), tok(B_WIDTH),
                  pl.BlockSpec((d, 2 * d), c2),
                  pl.BlockSpec((A_WIDTH, d), c2),
                  pl.BlockSpec((B_WIDTH, d), c2),
                  pl.BlockSpec((d, d), c2),
                  pl.BlockSpec((1, d), c2), pl.BlockSpec((1, d), c2)],
        out_specs=tok(d),
        out_shape=jax.ShapeDtypeStruct((bsz, s, d), F32),
        compiler_params=_params(2),
        name="merge",
    )(x, mod, ya, yb, w_gm.astype(BF16), w_br_a.astype(BF16), w_br_b.astype(BF16),
      w_mix_out.astype(BF16), ln_g.reshape(1, d), ln_b.reshape(1, d))


def kernel(x, c, w_ada, b_ada, ffn1_w_in, ffn1_w_out, ln1_g, ln1_b, w_mix_in, kv_norm_g, w_kv_up,
           cmp_pe_k, cmp_w1_k, cmp_w2_k, cmp_pe_v, cmp_w1_v, cmp_w2_v, w_br_a, w_br_b, w_mix_out,
           ln2_g, ln2_b, ffn2_w_in, ffn2_w_out, ln3_g, ln3_b):
    bsz, s, d = x.shape
    depth = w_ada.shape[0]
    alpha = (2.0 * depth) ** 0.25
    assert s % TOKEN_TILE == 0 and s % KEY_CHUNK == 0
    for l in range(depth):
        mod = _ada(c, w_ada[l], b_ada[l]).reshape(bsz, N_MOD, d)
        x = _ffn(x, mod, ffn1_w_in[l], ffn1_w_out[l], ln1_g[l], ln1_b[l], 0, alpha)
        qa_t, qb_t, qi_t, va_t, vs_t, vw_t, misc_t, ka, ki, ks, kw, cmp_in = _mixin(
            x, mod, w_mix_in[l], kv_norm_g[l], w_kv_up[l])
        ya = _dsa(qa_t, qi_t, misc_t, ki, ka, va_t)
        kcmp, vcmp_t = _compress(cmp_in, cmp_pe_k[l], cmp_w1_k[l], cmp_w2_k[l],
                                 cmp_pe_v[l], cmp_w1_v[l], cmp_w2_v[l])
        yb = _nsa(qb_t, misc_t, kcmp, vcmp_t, ks, vs_t, kw, vw_t)
        w_gm = w_mix_in[l][:, -2 * d:]
        x = _merge(x, mod, ya, yb, w_gm, w_br_a[l], w_br_b[l], w_mix_out[l], ln2_g[l], ln2_b[l], alpha)
        x = _ffn(x, mod, ffn2_w_in[l], ffn2_w_out[l], ln3_g[l], ln3_b[l], 6, alpha)
    return x
```

```python
import functools

import numpy as np
import jax
import jax.numpy as jnp
from jax import lax
from jax.experimental import pallas as pl
from jax.experimental.pallas import tpu as pltpu

F32 = jnp.float32
BF16 = jnp.bfloat16
I32 = jnp.int32

HEAD_DIM = 64
ROPE_THETA = 10000.0
A_HEADS = 8
A_WIDTH = A_HEADS * HEAD_DIM
A_KV_RANK = 128
A_IDX_HEADS = 4
A_IDX_DIM = 64
A_TOPK_MAX = 256
B_HEADS = 8
B_GROUPS = 2
B_REP = B_HEADS // B_GROUPS
B_WIDTH = B_HEADS * HEAD_DIM
B_KV_WIDTH = B_GROUPS * HEAD_DIM
B_CMP_STRIDE = 16
B_CMP_LEN = 2 * B_CMP_STRIDE
B_CMP_HIDDEN = 256
B_SLC_LEN = 64
B_SLC_TOP = 16
B_WINDOW = 512
N_MOD = 9
LN_EPS = 1e-5
NEG = -1e30
FORCE_BONUS = 1e4

LANES = 128
SUBLANES = 8
Q_BLOCK = LANES
KEY_CHUNK = 512
TOKEN_TILE = 512
FF_CHUNK = 256
PLANE_BITS = 32
ONES_ROWS = 16
VMEM_LIMIT = 56 * 1024 * 1024

NEG_KEY = int(np.array(NEG, np.float32).view(np.int32)) ^ 0x7FFFFFFF
INT_MIN = -(2 ** 31)
INT_MAX = 2 ** 31 - 1


def _dot(a, b):
    return jnp.dot(a, b, preferred_element_type=F32)


def _dot_nt(a, b):
    return lax.dot_general(a, b, (((1,), (1,)), ((), ())), preferred_element_type=F32)


def _layer_norm(y, g, b):
    mu = jnp.mean(y, axis=-1, keepdims=True)
    d = y - mu
    var = jnp.mean(d * d, axis=-1, keepdims=True)
    return d * lax.rsqrt(var + LN_EPS) * g + b


def _params(n_axes):
    return pltpu.CompilerParams(dimension_semantics=("arbitrary",) * n_axes,
                                vmem_limit_bytes=VMEM_LIMIT)


def _ada_kernel(c_ref, w_ref, b_ref, o_ref):
    c = c_ref[...]
    ca = c * jax.nn.sigmoid(c)
    o_ref[...] = jnp.dot(ca, w_ref[...], preferred_element_type=F32,
                         precision=lax.Precision.HIGHEST) + b_ref[...]


def _ada(c, w_ada, b_ada):
    bsz, d = c.shape
    n = w_ada.shape[1]
    tn = n // 8
    return pl.pallas_call(
        _ada_kernel,
        grid=(n // tn,),
        in_specs=[pl.BlockSpec((bsz, d), lambda j: (0, 0)),
                  pl.BlockSpec((d, tn), lambda j: (0, j)),
                  pl.BlockSpec((1, tn), lambda j: (0, j))],
        out_specs=pl.BlockSpec((bsz, tn), lambda j: (0, j)),
        out_shape=jax.ShapeDtypeStruct((bsz, n), F32),
        compiler_params=_params(1),
        name="ada",
    )(c, w_ada, b_ada.reshape(1, n))


def _ffn_kernel(x_ref, mod_ref, wa_ref, wb_ref, wo_ref, g_ref, b_ref, o_ref, u_ref, acc_ref,
                *, mod_base, n_chunks, alpha):
    x = x_ref[0]
    sh = mod_ref[0, mod_base:mod_base + 1, :]
    sc = mod_ref[0, mod_base + 1:mod_base + 2, :]
    gt = mod_ref[0, mod_base + 2:mod_base + 3, :]
    u_ref[...] = (x * (1.0 + sc) + sh).astype(BF16)
    acc_ref[...] = jnp.zeros_like(acc_ref)

    def body(j, carry):
        u = u_ref[...]
        a = _dot(u, wa_ref[j])
        b = _dot(u, wb_ref[j])
        h = (a * jax.nn.sigmoid(a) * b).astype(BF16)
        acc_ref[...] += _dot(h, wo_ref[j])
        return carry

    lax.fori_loop(0, n_chunks, body, 0)
    y = alpha * x + (0.5 * gt) * acc_ref[...]
    o_ref[0] = _layer_norm(y, g_ref[...], b_ref[...])


def _ffn(x, mod, w_in, w_out, ln_g, ln_b, mod_base, alpha):
    bsz, s, d = x.shape
    d_ff = w_out.shape[0]
    n_chunks = d_ff // FF_CHUNK
    tm = TOKEN_TILE
    wa = w_in[:, :d_ff].astype(BF16).reshape(d, n_chunks, FF_CHUNK).transpose(1, 0, 2)
    wb = w_in[:, d_ff:].astype(BF16).reshape(d, n_chunks, FF_CHUNK).transpose(1, 0, 2)
    wo = w_out.astype(BF16).reshape(n_chunks, FF_CHUNK, d)
    const3 = lambda b, i: (0, 0, 0)
    return pl.pallas_call(
        functools.partial(_ffn_kernel, mod_base=mod_base, n_chunks=n_chunks, alpha=alpha),
        grid=(bsz, s // tm),
        in_specs=[pl.BlockSpec((1, tm, d), lambda b, i: (b, i, 0)),
                  pl.BlockSpec((1, N_MOD, d), lambda b, i: (b, 0, 0)),
                  pl.BlockSpec((n_chunks, d, FF_CHUNK), const3),
                  pl.BlockSpec((n_chunks, d, FF_CHUNK), const3),
                  pl.BlockSpec((n_chunks, FF_CHUNK, d), const3),
                  pl.BlockSpec((1, d), lambda b, i: (0, 0)),
                  pl.BlockSpec((1, d), lambda b, i: (0, 0))],
        out_specs=pl.BlockSpec((1, tm, d), lambda b, i: (b, i, 0)),
        out_shape=jax.ShapeDtypeStruct((bsz, s, d), F32),
        scratch_shapes=[pltpu.VMEM((tm, d), BF16), pltpu.VMEM((tm, d), F32)],
        compiler_params=_params(2),
        name="ffn",
    )(x, mod, wa, wb, wo, ln_g.reshape(1, d), ln_b.reshape(1, d))


_S_KC = 0
_S_VC = _S_KC + B_KV_WIDTH
_S_KS = _S_VC + B_KV_WIDTH
_S_KW = _S_KS + B_KV_WIDTH
_S_LAT = _S_KW + B_KV_WIDTH
_S_KI = _S_LAT + A_KV_RANK
_P_STD = _S_KI + LANES
_T_QA = 0
_T_QB = _T_QA + A_WIDTH
_T_QI = _T_QB + B_WIDTH
_T_VS = _T_QI + A_IDX_HEADS * A_IDX_DIM
_T_VW = _T_VS + B_KV_WIDTH
_T_MISC = _T_VW + B_KV_WIDTH
_MISC_ROWS = 32
_MISC_WI = 3 * B_HEADS
_P_T = _T_MISC + _MISC_ROWS


def _mixin_kernel(x_ref, mod_ref, ws_ref, wt_ref, cos_ref, sin_ref, cost_ref, sint_ref,
                  kvg_ref, wk_ref, wvt_ref,
                  qat_ref, qbt_ref, qit_ref, vat_ref, vst_ref, vwt_ref, misct_ref,
                  ka_ref, ki_ref, ks_ref, kw_ref, cmp_ref, hs_ref, ht_ref):
    x = x_ref[0]
    sh = mod_ref[0, 3:4, :]
    sc = mod_ref[0, 4:5, :]
    u = (x * (1.0 + sc) + sh).astype(BF16)
    hs_ref[...] = _dot(u, ws_ref[...])
    ht_ref[...] = _dot_nt(wt_ref[...], u)

    tm = x.shape[0]
    half = HEAD_DIM // 2
    scale = HEAD_DIM ** -0.5 * float(np.log2(np.e))

    cos_t = cost_ref[...]
    sin_t = sint_ref[...]

    def rope_t(src0, dst_ref, dst0, mul):
        x1 = ht_ref[src0:src0 + half, :]
        x2 = ht_ref[src0 + half:src0 + HEAD_DIM, :]
        o1 = x1 * cos_t - x2 * sin_t
        o2 = x1 * sin_t + x2 * cos_t
        if mul != 1.0:
            o1 = o1 * mul
            o2 = o2 * mul
        dst_ref[0, dst0:dst0 + half, :] = o1.astype(dst_ref.dtype)
        dst_ref[0, dst0 + half:dst0 + HEAD_DIM, :] = o2.astype(dst_ref.dtype)

    for h in range(A_HEADS):
        rope_t(_T_QA + h * HEAD_DIM, qat_ref, h * HEAD_DIM, scale)
    for h in range(B_HEADS):
        rope_t(_T_QB + h * HEAD_DIM, qbt_ref, h * HEAD_DIM, scale)
    for h in range(A_IDX_HEADS):
        rope_t(_T_QI + h * A_IDX_DIM, qit_ref, h * A_IDX_DIM, 1.0)
    vst_ref[0] = ht_ref[_T_VS:_T_VS + B_KV_WIDTH, :].astype(BF16)
    vwt_ref[0] = ht_ref[_T_VW:_T_VW + B_KV_WIDTH, :].astype(BF16)
    misc = ht_ref[_T_MISC:_T_MISC + _MISC_ROWS, :]
    row = lax.broadcasted_iota(I32, misc.shape, 0)
    misct_ref[0] = jnp.where(row < _MISC_WI, jax.nn.sigmoid(misc), misc * (A_IDX_HEADS ** -0.5))

    cos = cos_ref[...]
    sin = sin_ref[...]
    lane = lax.broadcasted_iota(I32, (tm, LANES), 1)
    first_half = (lane & (HEAD_DIM - 1)) < half

    def rope(t):
        rot = jnp.where(first_half, pltpu.roll(t, LANES - half, 1), pltpu.roll(t, half, 1))
        return t * cos + rot * sin

    def chunk(c0):
        return hs_ref[:, c0:c0 + LANES]

    cmp_ref[0, :, :LANES] = rope(chunk(_S_KC))
    cmp_ref[0, :, LANES:] = chunk(_S_VC)
    ks = rope(chunk(_S_KS)).astype(BF16)
    kw = rope(chunk(_S_KW)).astype(BF16)
    for g in range(B_GROUPS):
        ks_ref[0, g] = ks[:, g * HEAD_DIM:(g + 1) * HEAD_DIM]
        kw_ref[0, g] = kw[:, g * HEAD_DIM:(g + 1) * HEAD_DIM]
    ki_ref[0] = rope(chunk(_S_KI))[:, :A_IDX_DIM].astype(BF16)

    lat = chunk(_S_LAT)
    latn = (lat * lax.rsqrt(jnp.mean(lat * lat, axis=-1, keepdims=True) + LN_EPS)
            * kvg_ref[...]).astype(BF16)
    ka_ref[0] = rope(_dot(latn, wk_ref[...]))[:, :HEAD_DIM].astype(BF16)
    vat_ref[0] = _dot_nt(wvt_ref[...], latn).astype(BF16)


def _rope_tables(s):
    half = HEAD_DIM // 2
    inv = ROPE_THETA ** (-jnp.arange(half, dtype=F32) / half)
    ang = jnp.arange(s).astype(F32)[:, None] * inv[None, :]
    cos = jnp.cos(ang)
    sin = jnp.sin(ang)
    reps = LANES // HEAD_DIM
    cos_l = jnp.tile(jnp.concatenate([cos, cos], axis=-1), (1, reps))
    sin_l = jnp.tile(jnp.concatenate([-sin, sin], axis=-1), (1, reps))
    return cos_l, sin_l, cos.T, sin.T


def _mixin(x, mod, w_mix_in, kv_norm_g, w_kv_up):
    bsz, s, d = x.shape
    tm = TOKEN_TILE
    split = np.cumsum([A_WIDTH, A_KV_RANK, A_IDX_HEADS * A_IDX_DIM, A_IDX_DIM, A_IDX_HEADS,
                       B_WIDTH, 6 * B_KV_WIDTH, 3 * B_HEADS]).tolist()
    w_qa, w_lat, w_qi, w_ki, w_wi, w_qb, w_kv6, w_gb, _ = jnp.split(w_mix_in, split, axis=-1)
    w_kc, w_vc, w_ks, w_vs, w_kw, w_vw = jnp.split(w_kv6, 6, axis=-1)
    zeros = lambda n: jnp.zeros((d, n), w_mix_in.dtype)
    w_std = jnp.concatenate([w_kc, w_vc, w_ks, w_kw, w_lat, w_ki, zeros(LANES - A_IDX_DIM)],
                            axis=-1).astype(BF16)
    w_t = jnp.concatenate([w_qa, w_qb, w_qi, w_vs, w_vw, w_gb, w_wi,
                           zeros(_MISC_ROWS - 3 * B_HEADS - A_IDX_HEADS)], axis=-1).T.astype(BF16)
    assert w_std.shape[1] == _P_STD and w_t.shape[0] == _P_T
    cos_l, sin_l, cos_t, sin_t = _rope_tables(s)
    w_k = jnp.concatenate([w_kv_up[:, :HEAD_DIM], jnp.zeros((A_KV_RANK, LANES - HEAD_DIM), F32)],
                          axis=-1).astype(BF16)
    w_vt = w_kv_up[:, HEAD_DIM:].T.astype(BF16)
    c2 = lambda b, i: (0, 0)
    feat = lambda rows: pl.BlockSpec((1, rows, tm), lambda b, i: (b, 0, i))
    tok = lambda width: pl.BlockSpec((1, tm, width), lambda b, i: (b, i, 0))
    grp = pl.BlockSpec((1, B_GROUPS, tm, HEAD_DIM), lambda b, i: (b, 0, i, 0))
    sds = jax.ShapeDtypeStruct
    out_shape = [sds((bsz, A_WIDTH, s), BF16),
                 sds((bsz, B_WIDTH, s), BF16),
                 sds((bsz, A_IDX_HEADS * A_IDX_DIM, s), BF16),
                 sds((bsz, HEAD_DIM, s), BF16),
                 sds((bsz, B_KV_WIDTH, s), BF16),
                 sds((bsz, B_KV_WIDTH, s), BF16),
                 sds((bsz, _MISC_ROWS, s), F32),
                 sds((bsz, s, HEAD_DIM), BF16),
                 sds((bsz, s, A_IDX_DIM), BF16),
                 sds((bsz, B_GROUPS, s, HEAD_DIM), BF16),
                 sds((bsz, B_GROUPS, s, HEAD_DIM), BF16),
                 sds((bsz, s, 2 * LANES), F32)]
    out_specs = [feat(A_WIDTH), feat(B_WIDTH), feat(A_IDX_HEADS * A_IDX_DIM), feat(HEAD_DIM),
                 feat(B_KV_WIDTH), feat(B_KV_WIDTH), feat(_MISC_ROWS),
                 tok(HEAD_DIM), tok(A_IDX_DIM), grp, grp, tok(2 * LANES)]
    return pl.pallas_call(
        _mixin_kernel,
        grid=(bsz, s // tm),
        in_specs=[pl.BlockSpec((1, tm, d), lambda b, i: (b, i, 0)),
                  pl.BlockSpec((1, N_MOD, d), lambda b, i: (b, 0, 0)),
                  pl.BlockSpec((d, _P_STD), c2),
                  pl.BlockSpec((_P_T, d), c2),
                  pl.BlockSpec((tm, LANES), lambda b, i: (i, 0)),
                  pl.BlockSpec((tm, LANES), lambda b, i: (i, 0)),
                  pl.BlockSpec((HEAD_DIM // 2, tm), lambda b, i: (0, i)),
                  pl.BlockSpec((HEAD_DIM // 2, tm), lambda b, i: (0, i)),
                  pl.BlockSpec((1, A_KV_RANK), c2),
                  pl.BlockSpec((A_KV_RANK, LANES), c2),
                  pl.BlockSpec((HEAD_DIM, A_KV_RANK), c2)],
        out_specs=out_specs,
        out_shape=out_shape,
        scratch_shapes=[pltpu.VMEM((tm, _P_STD), F32), pltpu.VMEM((_P_T, tm), F32)],
        compiler_params=_params(2),
        name="mixin",
    )(x, mod, w_std, w_t, cos_l, sin_l, cos_t, sin_t,
      kv_norm_g.reshape(1, A_KV_RANK), w_k, w_vt)


def _heads_on_lanes(qt, n_heads, dim):
    return jnp.concatenate([qt[h * dim:(h + 1) * dim, :] for h in range(n_heads)], axis=1)


def _fold_rows(x, op, tile=SUBLANES):
    while x.shape[0] % (2 * tile) == 0:
        h = x.shape[0] // 2
        x = op(x[:h], x[h:])
    return x


def _max_rows(x):
    return jnp.max(_fold_rows(x, jnp.maximum), axis=0, keepdims=True)


def _sum_rows(x):
    return jnp.sum(_fold_rows(x, jnp.add), axis=0, keepdims=True)


def _bit_transpose32(words):
    a = list(words)
    j, m = 16, 0x0000FFFF
    while j:
        k = 0
        while k < 32:
            t = (a[k] ^ lax.shift_right_logical(a[k + j], j)) & m
            a[k] = a[k] ^ t
            a[k + j] = a[k + j] ^ lax.shift_left(t, j)
            k = (k + j + 1) & ~j
        j >>= 1
        m ^= (m << j) & 0xFFFFFFFF
    return a


def _mask_cap(sel):
    return jnp.where(sel, jnp.inf, NEG)


def _with_ones_rows(v_t):
    return jnp.concatenate([v_t, jnp.ones((ONES_ROWS, v_t.shape[1]), v_t.dtype)], axis=0)


def _online_softmax_step_t(s_t, cap, v_ext, m_ref, acc_ref, n_heads):
    q = cap.shape[1]
    ps, alphas = [], []
    for h in range(n_heads):
        sl = slice(h * q, (h + 1) * q)
        sm = jnp.minimum(s_t[:, sl], cap)
        m_old = m_ref[:, sl]
        m_new = jnp.maximum(m_old, _max_rows(sm))
        ps.append(jnp.exp2(sm - m_new).astype(BF16))
        alphas.append(jnp.exp2(m_old - m_new))
        m_ref[:, sl] = m_new
    acc_ref[...] = (jnp.concatenate(alphas, axis=1) * acc_ref[...]
                    + _dot(v_ext, jnp.concatenate(ps, axis=1)))


def _pipelined_attention_t(n_chunks, scores_fn, cap_fn, v_fn, sa_ref, sb_ref, m_ref, acc_ref, n_heads):
    last = 2 * n_chunks - 1
    sa_ref[...] = scores_fn(0)

    def body(c, carry):
        i0 = 2 * c
        sb_ref[...] = scores_fn(i0 + 1)
        _online_softmax_step_t(sa_ref[...], cap_fn(i0), _with_ones_rows(v_fn(i0)), m_ref, acc_ref, n_heads)
        sa_ref[...] = scores_fn(jnp.minimum(i0 + 2, last))
        _online_softmax_step_t(sb_ref[...], cap_fn(i0 + 1), _with_ones_rows(v_fn(i0 + 1)),
                               m_ref, acc_ref, n_heads)
        return carry

    lax.fori_loop(0, n_chunks, body, 0)


def _finish_online_softmax_t(m_ref, acc_ref):
    d = acc_ref.shape[0] - ONES_ROWS
    return acc_ref[:d, :] * (jnp.where(m_ref[...] > NEG, 1.0, 0.0) / acc_ref[d:d + 1, :])


def _masked_softmax_t(s, cap):
    sm = jnp.minimum(s, cap)
    mx = _max_rows(sm)
    e = jnp.exp2(sm - mx)
    return e * (jnp.where(mx > NEG, 1.0, 0.0) / _sum_rows(e))


def _dsa_kernel(qat_ref, qit_ref, misct_ref, ki_ref, ka_ref, vat_ref, o_ref,
                key_ref, plane_ref, sa_ref, sb_ref, m_ref, acc_ref, *, topk, idx_bits):
    qb = Q_BLOCK
    kc = KEY_CHUNK
    q0 = pl.program_id(1) * qb
    n_chunks = lax.div(q0 + qb + kc - 1, kc)
    tq = q0 + lax.broadcasted_iota(I32, (1, qb), 1)
    row_k = lax.broadcasted_iota(I32, (kc, qb), 0)
    wi = misct_ref[0, _MISC_WI:_MISC_WI + A_IDX_HEADS, :]
    qi_t = qit_ref[0]
    qi_pairs = [_heads_on_lanes(qi_t[p * 2 * A_IDX_DIM:(p + 1) * 2 * A_IDX_DIM], 2, A_IDX_DIM)
                for p in range(A_IDX_HEADS // 2)]

    @pl.when((pl.program_id(0) == 0) & (pl.program_id(1) == 0))
    def _():
        plane_ref[...] = jnp.zeros_like(plane_ref)

    def score_chunk(c, carry):
        k0 = pl.multiple_of(c * kc, kc)
        ki = ki_ref[0, pl.ds(k0, kc), :]
        acc = jnp.zeros((kc, qb), F32)
        for p in range(A_IDX_HEADS // 2):
            lg = _dot(ki, qi_pairs[p])
            for j in range(2):
                h = 2 * p + j
                acc = acc + wi[h:h + 1, :] * jnp.maximum(lg[:, j * qb:(j + 1) * qb], 0.0)
        sc = jnp.where(k0 + row_k <= tq, acc, NEG)
        sc = jnp.where(sc == 0.0, 0.0, sc)
        bits = pltpu.bitcast(sc, I32)
        key = jnp.where(bits < 0, bits ^ 0x7FFFFFFF, bits)
        key_ref[pl.ds(k0, kc), :] = key
        ukey = key ^ INT_MIN
        group = PLANE_BITS * SUBLANES
        for g in range(kc // group):
            rows = [ukey[g * group + v * SUBLANES:g * group + (v + 1) * SUBLANES, :]
                    for v in range(PLANE_BITS)]
            base = pl.multiple_of(c * (kc // PLANE_BITS) + g * SUBLANES, SUBLANES)
            for b, plane in enumerate(_bit_transpose32(rows)):
                plane_ref[b, pl.ds(base, SUBLANES), :] = plane
        return carry

    lax.fori_loop(0, n_chunks, score_chunk, 0)

    wide = 2 * kc
    n_wide = lax.div(q0 + qb + wide - 1, wide)
    row_w = lax.broadcasted_iota(I32, (wide, qb), 0)

    @pl.when(n_wide * 2 > n_chunks)
    def _():
        key_ref[pl.ds(pl.multiple_of(n_chunks * kc, kc), kc), :] = jnp.full((kc, qb), INT_MIN, I32)

    def count(pred):
        def body(c, acc):
            k0 = pl.multiple_of(c * wide, wide)
            hit = jnp.where(pred(key_ref[pl.ds(k0, wide), :], k0 + row_w), 1.0, 0.0)
            return acc + _fold_rows(hit, jnp.add)
        acc = lax.fori_loop(0, n_wide, body, jnp.zeros((SUBLANES, qb), F32))
        return jnp.sum(acc, axis=0, keepdims=True)

    n_words = n_chunks * (kc // PLANE_BITS)
    word_row = lax.broadcasted_iota(I32, (plane_ref.shape[1], qb), 0)

    def search(it, carry):
        gt, eq, thr_u, cnt_thr = carry
        hit = eq & plane_ref[it]
        ge = gt | hit
        ones = _fold_rows(lax.population_count(ge), jnp.add).astype(F32)
        cnt = jnp.sum(ones, axis=0, keepdims=True)
        ok = cnt >= topk
        bit = lax.shift_left(jnp.int32(1), PLANE_BITS - 1 - it)
        return (jnp.where(ok, gt, ge), jnp.where(ok, hit, eq ^ hit),
                jnp.where(ok, thr_u | bit, thr_u), jnp.where(ok, cnt, cnt_thr))

    eq0 = jnp.where(word_row < n_words, jnp.int32(-1), jnp.int32(0))
    cnt0 = jnp.full((1, qb), 1.0, F32) * (n_chunks * kc).astype(F32)
    _, _, thr_u, cnt_thr = lax.fori_loop(
        0, PLANE_BITS, search, (jnp.zeros_like(eq0), eq0, jnp.zeros((1, qb), I32), cnt0))
    thr = thr_u ^ INT_MIN

    has_tie = jnp.max(jnp.where((cnt_thr > topk) & (thr > NEG_KEY), 1.0, 0.0)) > 0.0

    def tie_search():
        need = topk - count(lambda key, pos: key > thr)

        def step(it, last):
            cand = last + lax.shift_left(jnp.int32(1), idx_bits - 1 - it)
            below = count(lambda key, pos: (key == thr) & (pos < cand))
            return jnp.where(below < need, cand, last)

        return lax.fori_loop(0, idx_bits, step, jnp.zeros((1, qb), I32))

    last_eq = lax.cond(has_tie, tie_search, lambda: jnp.full((1, qb), INT_MAX, I32))

    m_ref[...] = jnp.full_like(m_ref, NEG)
    acc_ref[...] = jnp.zeros_like(acc_ref)
    q_all = _heads_on_lanes(qat_ref[0], A_HEADS, HEAD_DIM)

    half = kc // 2
    row_h = lax.broadcasted_iota(I32, (half, qb), 0)

    def keys_at(i):
        return pl.ds(pl.multiple_of(i * half, half), half)

    def cap_fn(i):
        key = key_ref[keys_at(i), :]
        pos = i * half + row_h
        return _mask_cap(((key > thr) | ((key == thr) & (pos <= last_eq))) & (pos <= tq))

    _pipelined_attention_t(n_chunks,
                           lambda i: _dot(ka_ref[0, keys_at(i), :], q_all),
                           cap_fn,
                           lambda i: vat_ref[0, :, keys_at(i)],
                           sa_ref, sb_ref, m_ref, acc_ref, A_HEADS)
    o_t = _finish_online_softmax_t(m_ref, acc_ref)
    o_t = jnp.concatenate([o_t[:, h * qb:(h + 1) * qb] for h in range(A_HEADS)], axis=0)
    o_ref[0] = o_t.T.astype(o_ref.dtype)


def _dsa(qa_t, qi_t, misc_t, ki, ka, va_t):
    bsz, _, s = qa_t.shape
    qb = Q_BLOCK
    assert s % (2 * KEY_CHUNK) == 0 and KEY_CHUNK >= A_TOPK_MAX
    topk = min(A_TOPK_MAX, s // 4)
    idx_bits = int(np.ceil(np.log2(s)))
    feat = lambda rows: pl.BlockSpec((1, rows, qb), lambda b, n: (b, 0, n))
    return pl.pallas_call(
        functools.partial(_dsa_kernel, topk=topk, idx_bits=idx_bits),
        grid=(bsz, s // qb),
        in_specs=[feat(A_WIDTH), feat(A_IDX_HEADS * A_IDX_DIM), feat(_MISC_ROWS),
                  pl.BlockSpec((1, s, A_IDX_DIM), lambda b, n: (b, 0, 0)),
                  pl.BlockSpec((1, s, HEAD_DIM), lambda b, n: (b, 0, 0)),
                  pl.BlockSpec((1, HEAD_DIM, s), lambda b, n: (b, 0, 0))],
        out_specs=pl.BlockSpec((1, qb, A_WIDTH), lambda b, n: (b, n, 0)),
        out_shape=jax.ShapeDtypeStruct((bsz, s, A_WIDTH), BF16),
        scratch_shapes=[pltpu.VMEM((s, qb), I32),
                        pltpu.VMEM((PLANE_BITS, s // PLANE_BITS, qb), I32),
                        pltpu.VMEM((KEY_CHUNK // 2, A_HEADS * qb), F32),
                        pltpu.VMEM((KEY_CHUNK // 2, A_HEADS * qb), F32),
                        pltpu.VMEM((1, A_HEADS * qb), F32),
                        pltpu.VMEM((HEAD_DIM + ONES_ROWS, A_HEADS * qb), F32)],
        compiler_params=_params(2),
        name="dsa",
    )(qa_t, qi_t, misc_t, ki, ka, va_t)


def _compress_kernel(xk_ref, xv_ref, pek_ref, pev_ref, w1k_ref, w1v_ref, w2k_ref, w2vt_ref,
                     ok_ref, ovt_ref):
    def hidden(x_ref, pe_ref, w1_ref):
        x = x_ref[0, 0]
        n = x.shape[0]
        nxt = pltpu.roll(x, n - 1, 0)
        a = (x + pe_ref[0:1, :]).astype(BF16)
        b = (nxt + pe_ref[1:2, :]).astype(BF16)
        hid = _dot(a, w1_ref[0]) + _dot(b, w1_ref[1])
        return (hid * jax.nn.sigmoid(hid)).astype(BF16)

    ok_ref[0, 0] = _dot(hidden(xk_ref, pek_ref, w1k_ref), w2k_ref[...]).astype(ok_ref.dtype)
    ovt_ref[0, 0] = _dot_nt(w2vt_ref[...], hidden(xv_ref, pev_ref, w1v_ref)).astype(ovt_ref.dtype)


def _compress(cmp_in, pe_k, w1_k, w2_k, pe_v, w1_v, w2_v):
    bsz, s, _ = cmp_in.shape
    nch = s // B_CMP_STRIDE
    cw = B_CMP_STRIDE * HEAD_DIM

    def chunks(t):
        t = t.reshape(bsz, nch, B_CMP_STRIDE, B_GROUPS, HEAD_DIM)
        return t.transpose(0, 3, 1, 2, 4).reshape(bsz, B_GROUPS, nch, cw)

    xk = chunks(cmp_in[:, :, :LANES])
    xv = chunks(cmp_in[:, :, LANES:])
    xspec = pl.BlockSpec((1, 1, nch, cw), lambda b, g: (b, g, 0, 0))
    c2 = lambda b, g: (0, 0)
    c3 = lambda b, g: (0, 0, 0)
    return pl.pallas_call(
        _compress_kernel,
        grid=(bsz, B_GROUPS),
        in_specs=[xspec, xspec,
                  pl.BlockSpec((2, cw), c2), pl.BlockSpec((2, cw), c2),
                  pl.BlockSpec((2, cw, B_CMP_HIDDEN), c3), pl.BlockSpec((2, cw, B_CMP_HIDDEN), c3),
                  pl.BlockSpec((B_CMP_HIDDEN, HEAD_DIM), c2), pl.BlockSpec((HEAD_DIM, B_CMP_HIDDEN), c2)],
        out_specs=[pl.BlockSpec((1, 1, nch, HEAD_DIM), lambda b, g: (b, g, 0, 0)),
                   pl.BlockSpec((1, 1, HEAD_DIM, nch), lambda b, g: (b, g, 0, 0))],
        out_shape=[jax.ShapeDtypeStruct((bsz, B_GROUPS, nch, HEAD_DIM), BF16),
                   jax.ShapeDtypeStruct((bsz, B_GROUPS, HEAD_DIM, nch), BF16)],
        compiler_params=_params(2),
        name="compress",
    )(xk, xv, pe_k.reshape(2, cw), pe_v.reshape(2, cw),
      w1_k.astype(BF16).reshape(2, cw, B_CMP_HIDDEN), w1_v.astype(BF16).reshape(2, cw, B_CMP_HIDDEN),
      w2_k.astype(BF16), w2_v.T.astype(BF16))


def _nsa_kernel(qbt_ref, misct_ref, kcmp_ref, vcmpt_ref, ks_ref, vst_ref, kw_ref, vwt_ref, ovlt_ref,
                o_ref, chosen_ref, sa_ref, sb_ref, m_ref, acc_ref, out_ref, *, n_slc, top_n):
    qb = Q_BLOCK
    kc = KEY_CHUNK
    rep = B_REP
    slc_shift = int(np.log2(B_SLC_LEN))
    q0 = pl.program_id(1) * qb
    n_chunks = lax.div(q0 + qb + kc - 1, kc)
    tq = q0 + lax.broadcasted_iota(I32, (1, qb), 1)
    gates = misct_ref[0]
    n_cmp = kcmp_ref.shape[2]
    n_rows = chosen_ref.shape[0]
    cmp_end = lax.broadcasted_iota(I32, (n_cmp, qb), 0) * B_CMP_STRIDE + (B_CMP_LEN - 1)
    cap_c = _mask_cap(cmp_end <= tq)
    jj = lax.broadcasted_iota(I32, (n_rows, qb), 0)
    jj_f = jj.astype(F32)
    blk_q = lax.shift_right_logical(tq, slc_shift)
    forced = (jj == 0) | (jj == blk_q) | (jj == blk_q - 1)
    half = kc // 2
    row_h = lax.broadcasted_iota(I32, (half, qb), 0)
    w0 = pl.multiple_of(jnp.maximum(q0 - B_WINDOW, 0), qb)
    wlen = B_WINDOW + qb
    pos_w = w0 + lax.broadcasted_iota(I32, (wlen, qb), 0)
    cap_w = _mask_cap((pos_w <= tq) & (pos_w > tq - B_WINDOW))

    for g in range(B_GROUPS):
        q4 = _heads_on_lanes(qbt_ref[0, g * rep * HEAD_DIM:(g + 1) * rep * HEAD_DIM, :], rep, HEAD_DIM)
        vrows = slice(g * HEAD_DIM, (g + 1) * HEAD_DIM)

        s_c = _dot(kcmp_ref[0, g], q4)
        p_c = [_masked_softmax_t(s_c[:, r * qb:(r + 1) * qb], cap_c) for r in range(rep)]
        o_c = _dot(vcmpt_ref[0, g], jnp.concatenate([p.astype(BF16) for p in p_c], axis=1))

        p_sum = p_c[0]
        for r in range(1, rep):
            p_sum = p_sum + p_c[r]
        hi = p_sum.astype(BF16)
        r1 = p_sum - hi.astype(F32)
        mid = r1.astype(BF16)
        lo = (r1 - mid.astype(F32)).astype(BF16)
        ovl = ovlt_ref[...]
        imp = _dot(ovl, hi) + _dot(ovl, mid) + _dot(ovl, lo)
        val = jnp.where(jj <= blk_q, imp + FORCE_BONUS * jnp.where(forced, 1.0, 0.0), NEG)
        val = jnp.where(jj < n_slc, val, -jnp.inf)

        def pick(i, carry):
            cur, chosen = carry
            mx = jnp.max(cur, axis=0, keepdims=True)
            first = jnp.min(jnp.where(cur == mx, jj_f, float(n_rows)), axis=0, keepdims=True)
            hit = jj_f == first
            return jnp.where(hit, -jnp.inf, cur), jnp.where(hit, 1.0, chosen)

        _, chosen = lax.fori_loop(0, top_n, pick, (val, jnp.zeros((n_rows, qb), F32)))
        chosen_ref[...] = chosen

        m_ref[...] = jnp.full_like(m_ref, NEG)
        acc_ref[...] = jnp.zeros_like(acc_ref)

        def keys_at(i):
            return pl.ds(pl.multiple_of(i * half, half), half)

        def cap_fn(i):
            j0 = i * (half // B_SLC_LEN)
            blocks = [jnp.broadcast_to(chosen_ref[pl.ds(j0 + b, 1), :], (B_SLC_LEN, qb))
                      for b in range(half // B_SLC_LEN)]
            return _mask_cap((jnp.concatenate(blocks, axis=0) > 0.5) & (i * half + row_h <= tq))

        _pipelined_attention_t(n_chunks,
                               lambda i: _dot(ks_ref[0, g, keys_at(i), :], q4),
                               cap_fn,
                               lambda i: vst_ref[0, vrows, keys_at(i)],
                               sa_ref, sb_ref, m_ref, acc_ref, rep)
        o_s = _finish_online_softmax_t(m_ref, acc_ref)

        s_w = _dot(kw_ref[0, g, pl.ds(w0, wlen), :], q4)
        p_w = [_masked_softmax_t(s_w[:, r * qb:(r + 1) * qb], cap_w).astype(BF16) for r in range(rep)]
        o_w = _dot(vwt_ref[0, vrows, pl.ds(w0, wlen)], jnp.concatenate(p_w, axis=1))

        for r in range(rep):
            h = g * rep + r
            sl = slice(r * qb, (r + 1) * qb)
            out_ref[h * HEAD_DIM:(h + 1) * HEAD_DIM, :] = (
                gates[h:h + 1, :] * o_c[:, sl] + gates[B_HEADS + h:B_HEADS + h + 1, :] * o_s[:, sl]
                + gates[2 * B_HEADS + h:2 * B_HEADS + h + 1, :] * o_w[:, sl])

    o_ref[0] = out_ref[...].T.astype(o_ref.dtype)


def _nsa(qb_t, misc_t, kcmp, vcmp_t, ks, vs_t, kw, vw_t):
    bsz, _, s = qb_t.shape
    qblk = Q_BLOCK
    n_cmp = kcmp.shape[2]
    n_slc = s // B_SLC_LEN
    top_n = min(B_SLC_TOP, n_slc)
    n_rows = -(-n_slc // SUBLANES) * SUBLANES
    assert s % KEY_CHUNK == 0 and s >= B_WINDOW + qblk and KEY_CHUNK % B_SLC_LEN == 0
    slc_start = np.arange(n_rows)[:, None] * B_SLC_LEN
    cmp_start = np.arange(n_cmp)[None, :] * B_CMP_STRIDE
    overlap_t = ((cmp_start < slc_start + B_SLC_LEN) & (slc_start < cmp_start + B_CMP_LEN)
                 & (np.arange(n_rows)[:, None] < n_slc))
    overlap_t = jnp.asarray(overlap_t, BF16)
    feat = lambda rows: pl.BlockSpec((1, rows, qblk), lambda b, n: (b, 0, n))
    k_spec = pl.BlockSpec((1, B_GROUPS, s, HEAD_DIM), lambda b, n: (b, 0, 0, 0))
    vt_spec = pl.BlockSpec((1, B_KV_WIDTH, s), lambda b, n: (b, 0, 0))
    return pl.pallas_call(
        functools.partial(_nsa_kernel, n_slc=n_slc, top_n=top_n),
        grid=(bsz, s // qblk),
        in_specs=[feat(B_WIDTH), feat(_MISC_ROWS),
                  pl.BlockSpec((1, B_GROUPS, n_cmp, HEAD_DIM), lambda b, n: (b, 0, 0, 0)),
                  pl.BlockSpec((1, B_GROUPS, HEAD_DIM, n_cmp), lambda b, n: (b, 0, 0, 0)),
                  k_spec, vt_spec, k_spec, vt_spec,
                  pl.BlockSpec((n_rows, n_cmp), lambda b, n: (0, 0))],
        out_specs=pl.BlockSpec((1, qblk, B_WIDTH), lambda b, n: (b, n, 0)),
        out_shape=jax.ShapeDtypeStruct((bsz, s, B_WIDTH), BF16),
        scratch_shapes=[pltpu.VMEM((n_rows, qblk), F32),
                        pltpu.VMEM((KEY_CHUNK // 2, B_REP * qblk), F32),
                        pltpu.VMEM((KEY_CHUNK // 2, B_REP * qblk), F32),
                        pltpu.VMEM((1, B_REP * qblk), F32),
                        pltpu.VMEM((HEAD_DIM + ONES_ROWS, B_REP * qblk), F32),
                        pltpu.VMEM((B_WIDTH, qblk), F32)],
        compiler_params=_params(2),
        name="nsa",
    )(qb_t, misc_t, kcmp, vcmp_t, ks, vs_t, kw, vw_t, overlap_t)


def _merge_kernel(x_ref, mod_ref, ya_ref, yb_ref, wgm_ref, wa_ref, wb_ref, wo_ref, g_ref, b_ref,
                  o_ref, *, alpha):
    x = x_ref[0]
    d = x.shape[-1]
    sh = mod_ref[0, 3:4, :]
    sc = mod_ref[0, 4:5, :]
    gt = mod_ref[0, 5:6, :]
    u = (x * (1.0 + sc) + sh).astype(BF16)
    gm = jax.nn.sigmoid(_dot(u, wgm_ref[...]))
    merged = gm[:, :d] * _dot(ya_ref[0], wa_ref[...]) + gm[:, d:] * _dot(yb_ref[0], wb_ref[...])
    mix = _dot(merged.astype(BF16), wo_ref[...])
    o_ref[0] = _layer_norm(alpha * x + gt * mix, g_ref[...], b_ref[...])


def _merge(x, mod, ya, yb, w_gm, w_br_a, w_br_b, w_mix_out, ln_g, ln_b, alpha):
    bsz, s, d = x.shape
    tm = TOKEN_TILE
    c2 = lambda b, i: (0, 0)
    tok = lambda width: pl.BlockSpec((1, tm, width), lambda b, i: (b, i, 0))
    return pl.pallas_call(
        functools.partial(_merge_kernel, alpha=alpha),
        grid=(bsz, s // tm),
        in_specs=[tok(d),
                  pl.BlockSpec((1, N_MOD, d), lambda b, i: (b, 0, 0)),
                  tok(A_WIDTH), tok(B_WIDTH),
                  pl.BlockSpec((d, 2 * d), c2),
                  pl.BlockSpec((A_WIDTH, d), c2),
                  pl.BlockSpec((B_WIDTH, d), c2),
                  pl.BlockSpec((d, d), c2),
                  pl.BlockSpec((1, d), c2), pl.BlockSpec((1, d), c2)],
        out_specs=tok(d),
        out_shape=jax.ShapeDtypeStruct((bsz, s, d), F32),
        compiler_params=_params(2),
        name="merge",
    )(x, mod, ya, yb, w_gm.astype(BF16), w_br_a.astype(BF16), w_br_b.astype(BF16),
      w_mix_out.astype(BF16), ln_g.reshape(1, d), ln_b.reshape(1, d))


def kernel(x, c, w_ada, b_ada, ffn1_w_in, ffn1_w_out, ln1_g, ln1_b, w_mix_in, kv_norm_g, w_kv_up,
           cmp_pe_k, cmp_w1_k, cmp_w2_k, cmp_pe_v, cmp_w1_v, cmp_w2_v, w_br_a, w_br_b, w_mix_out,
           ln2_g, ln2_b, ffn2_w_in, ffn2_w_out, ln3_g, ln3_b):
    bsz, s, d = x.shape
    depth = w_ada.shape[0]
    alpha = (2.0 * depth) ** 0.25
    assert s % TOKEN_TILE == 0 and s % KEY_CHUNK == 0
    for l in range(depth):
        mod = _ada(c, w_ada[l], b_ada[l]).reshape(bsz, N_MOD, d)
        x = _ffn(x, mod, ffn1_w_in[l], ffn1_w_out[l], ln1_g[l], ln1_b[l], 0, alpha)
        qa_t, qb_t, qi_t, va_t, vs_t, vw_t, misc_t, ka, ki, ks, kw, cmp_in = _mixin(
            x, mod, w_mix_in[l], kv_norm_g[l], w_kv_up[l])
        ya = _dsa(qa_t, qi_t, misc_t, ki, ka, va_t)
        kcmp, vcmp_t = _compress(cmp_in, cmp_pe_k[l], cmp_w1_k[l], cmp_w2_k[l],
                                 cmp_pe_v[l], cmp_w1_v[l], cmp_w2_v[l])
        yb = _nsa(qb_t, misc_t, kcmp, vcmp_t, ks, vs_t, kw, vw_t)
        w_gm = w_mix_in[l][:, -2 * d:]
        x = _merge(x, mod, ya, yb, w_gm, w_br_a[l], w_br_b[l], w_mix_out[l], ln2_g[l], ln2_b[l], alpha)
        x = _ffn(x, mod, ffn2_w_in[l], ffn2_w_out[l], ln3_g[l], ln3_b[l], 6, alpha)
    return x
```

```python
import functools

import numpy as np
import jax
import jax.numpy as jnp
from jax import lax
from jax.experimental import pallas as pl
from jax.experimental.pallas import tpu as pltpu

F32 = jnp.float32
BF16 = jnp.bfloat16
I32 = jnp.int32

HEAD_DIM = 64
ROPE_THETA = 10000.0
A_HEADS = 8
A_WIDTH = A_HEADS * HEAD_DIM
A_KV_RANK = 128
A_IDX_HEADS = 4
A_IDX_DIM = 64
A_TOPK_MAX = 256
B_HEADS = 8
B_GROUPS = 2
B_REP = B_HEADS // B_GROUPS
B_WIDTH = B_HEADS * HEAD_DIM
B_KV_WIDTH = B_GROUPS * HEAD_DIM
B_CMP_STRIDE = 16
B_CMP_LEN = 2 * B_CMP_STRIDE
B_CMP_HIDDEN = 256
B_SLC_LEN = 64
B_SLC_TOP = 16
B_WINDOW = 512
N_MOD = 9
LN_EPS = 1e-5
NEG = -1e30
FORCE_BONUS = 1e4

LANES = 128
SUBLANES = 8
Q_BLOCK = LANES
KEY_CHUNK = 512
TOKEN_TILE = 512
FF_CHUNK = 256
PLANE_BITS = 32
ONES_ROWS = 16
VMEM_LIMIT = 56 * 1024 * 1024

NEG_KEY = int(np.array(NEG, np.float32).view(np.int32)) ^ 0x7FFFFFFF
INT_MIN = -(2 ** 31)
INT_MAX = 2 ** 31 - 1


def _dot(a, b):
    return jnp.dot(a, b, preferred_element_type=F32)


def _dot_nt(a, b):
    return lax.dot_general(a, b, (((1,), (1,)), ((), ())), preferred_element_type=F32)


def _layer_norm(y, g, b):
    mu = jnp.mean(y, axis=-1, keepdims=True)
    d = y - mu
    var = jnp.mean(d * d, axis=-1, keepdims=True)
    return d * lax.rsqrt(var + LN_EPS) * g + b


def _params(n_axes):
    return pltpu.CompilerParams(dimension_semantics=("arbitrary",) * n_axes,
                                vmem_limit_bytes=VMEM_LIMIT)


def _ada_kernel(c_ref, w_ref, b_ref, o_ref):
    c = c_ref[...]
    ca = c * jax.nn.sigmoid(c)
    o_ref[...] = jnp.dot(ca, w_ref[...], preferred_element_type=F32,
                         precision=lax.Precision.HIGHEST) + b_ref[...]


def _ada(c, w_ada, b_ada):
    bsz, d = c.shape
    n = w_ada.shape[1]
    tn = n // 8
    return pl.pallas_call(
        _ada_kernel,
        grid=(n // tn,),
        in_specs=[pl.BlockSpec((bsz, d), lambda j: (0, 0)),
                  pl.BlockSpec((d, tn), lambda j: (0, j)),
                  pl.BlockSpec((1, tn), lambda j: (0, j))],
        out_specs=pl.BlockSpec((bsz, tn), lambda j: (0, j)),
        out_shape=jax.ShapeDtypeStruct((bsz, n), F32),
        compiler_params=_params(1),
        name="ada",
    )(c, w_ada, b_ada.reshape(1, n))


def _ffn_kernel(x_ref, mod_ref, wa_ref, wb_ref, wo_ref, g_ref, b_ref, o_ref, u_ref, acc_ref,
                *, mod_base, n_chunks, alpha):
    x = x_ref[0]
    sh = mod_ref[0, mod_base:mod_base + 1, :]
    sc = mod_ref[0, mod_base + 1:mod_base + 2, :]
    gt = mod_ref[0, mod_base + 2:mod_base + 3, :]
    u_ref[...] = (x * (1.0 + sc) + sh).astype(BF16)
    for j in range(n_chunks):
        u = u_ref[...]
        a = _dot(u, wa_ref[j])
        b = _dot(u, wb_ref[j])
        h = (a * jax.nn.sigmoid(a) * b).astype(BF16)
        d = _dot(h, wo_ref[j])
        if j == 0:
            acc_ref[...] = d
        else:
            acc_ref[...] += d
    y = alpha * x + (0.5 * gt) * acc_ref[...]
    o_ref[0] = _layer_norm(y, g_ref[...], b_ref[...])


def _ffn(x, mod, w_in, w_out, ln_g, ln_b, mod_base, alpha):
    bsz, s, d = x.shape
    d_ff = w_out.shape[0]
    n_chunks = d_ff // FF_CHUNK
    tm = TOKEN_TILE
    wa = w_in[:, :d_ff].astype(BF16).reshape(d, n_chunks, FF_CHUNK).transpose(1, 0, 2)
    wb = w_in[:, d_ff:].astype(BF16).reshape(d, n_chunks, FF_CHUNK).transpose(1, 0, 2)
    wo = w_out.astype(BF16).reshape(n_chunks, FF_CHUNK, d)
    const3 = lambda b, i: (0, 0, 0)
    return pl.pallas_call(
        functools.partial(_ffn_kernel, mod_base=mod_base, n_chunks=n_chunks, alpha=alpha),
        grid=(bsz, s // tm),
        in_specs=[pl.BlockSpec((1, tm, d), lambda b, i: (b, i, 0)),
                  pl.BlockSpec((1, N_MOD, d), lambda b, i: (b, 0, 0)),
                  pl.BlockSpec((n_chunks, d, FF_CHUNK), const3),
                  pl.BlockSpec((n_chunks, d, FF_CHUNK), const3),
                  pl.BlockSpec((n_chunks, FF_CHUNK, d), const3),
                  pl.BlockSpec((1, d), lambda b, i: (0, 0)),
                  pl.BlockSpec((1, d), lambda b, i: (0, 0))],
        out_specs=pl.BlockSpec((1, tm, d), lambda b, i: (b, i, 0)),
        out_shape=jax.ShapeDtypeStruct((bsz, s, d), F32),
        scratch_shapes=[pltpu.VMEM((tm, d), BF16), pltpu.VMEM((tm, d), F32)],
        compiler_params=_params(2),
        name="ffn",
    )(x, mod, wa, wb, wo, ln_g.reshape(1, d), ln_b.reshape(1, d))


_S_KC = 0
_S_VC = _S_KC + B_KV_WIDTH
_S_KS = _S_VC + B_KV_WIDTH
_S_KW = _S_KS + B_KV_WIDTH
_S_LAT = _S_KW + B_KV_WIDTH
_S_KI = _S_LAT + A_KV_RANK
_P_STD = _S_KI + LANES
_T_QA = 0
_T_QB = _T_QA + A_WIDTH
_T_QI = _T_QB + B_WIDTH
_T_VS = _T_QI + A_IDX_HEADS * A_IDX_DIM
_T_VW = _T_VS + B_KV_WIDTH
_T_MISC = _T_VW + B_KV_WIDTH
_MISC_ROWS = 32
_MISC_WI = 3 * B_HEADS
_P_T = _T_MISC + _MISC_ROWS


def _mixin_kernel(x_ref, mod_ref, ws_ref, wt_ref, cos_ref, sin_ref, cost_ref, sint_ref,
                  kvg_ref, wk_ref, wvt_ref,
                  qat_ref, qbt_ref, qit_ref, vat_ref, vst_ref, vwt_ref, misct_ref,
                  ka_ref, ki_ref, ks_ref, kw_ref, cmp_ref, hs_ref, ht_ref):
    x = x_ref[0]
    sh = mod_ref[0, 3:4, :]
    sc = mod_ref[0, 4:5, :]
    u = (x * (1.0 + sc) + sh).astype(BF16)
    hs_ref[...] = _dot(u, ws_ref[...])
    ht_ref[...] = _dot_nt(wt_ref[...], u)

    tm = x.shape[0]
    half = HEAD_DIM // 2
    scale = HEAD_DIM ** -0.5 * float(np.log2(np.e))

    cos_t = cost_ref[...]
    sin_t = sint_ref[...]

    def rope_t(src0, dst_ref, dst0, mul):
        x1 = ht_ref[src0:src0 + half, :]
        x2 = ht_ref[src0 + half:src0 + HEAD_DIM, :]
        o1 = x1 * cos_t - x2 * sin_t
        o2 = x1 * sin_t + x2 * cos_t
        if mul != 1.0:
            o1 = o1 * mul
            o2 = o2 * mul
        dst_ref[0, dst0:dst0 + half, :] = o1.astype(dst_ref.dtype)
        dst_ref[0, dst0 + half:dst0 + HEAD_DIM, :] = o2.astype(dst_ref.dtype)

    for h in range(A_HEADS):
        rope_t(_T_QA + h * HEAD_DIM, qat_ref, h * HEAD_DIM, scale)
    for h in range(B_HEADS):
        rope_t(_T_QB + h * HEAD_DIM, qbt_ref, h * HEAD_DIM, scale)
    for h in range(A_IDX_HEADS):
        rope_t(_T_QI + h * A_IDX_DIM, qit_ref, h * A_IDX_DIM, 1.0)
    vst_ref[0] = ht_ref[_T_VS:_T_VS + B_KV_WIDTH, :].astype(BF16)
    vwt_ref[0] = ht_ref[_T_VW:_T_VW + B_KV_WIDTH, :].astype(BF16)
    misc = ht_ref[_T_MISC:_T_MISC + _MISC_ROWS, :]
    row = lax.broadcasted_iota(I32, misc.shape, 0)
    misct_ref[0] = jnp.where(row < _MISC_WI, jax.nn.sigmoid(misc), misc * (A_IDX_HEADS ** -0.5))

    cos = cos_ref[...]
    sin = sin_ref[...]
    lane = lax.broadcasted_iota(I32, (tm, LANES), 1)
    first_half = (lane & (HEAD_DIM - 1)) < half

    def rope(t):
        rot = jnp.where(first_half, pltpu.roll(t, LANES - half, 1), pltpu.roll(t, half, 1))
        return t * cos + rot * sin

    def chunk(c0):
        return hs_ref[:, c0:c0 + LANES]

    cmp_ref[0, :, :LANES] = rope(chunk(_S_KC))
    cmp_ref[0, :, LANES:] = chunk(_S_VC)
    ks = rope(chunk(_S_KS)).astype(BF16)
    kw = rope(chunk(_S_KW)).astype(BF16)
    for g in range(B_GROUPS):
        ks_ref[0, g] = ks[:, g * HEAD_DIM:(g + 1) * HEAD_DIM]
        kw_ref[0, g] = kw[:, g * HEAD_DIM:(g + 1) * HEAD_DIM]
    ki_ref[0] = rope(chunk(_S_KI))[:, :A_IDX_DIM].astype(BF16)

    lat = chunk(_S_LAT)
    latn = (lat * lax.rsqrt(jnp.mean(lat * lat, axis=-1, keepdims=True) + LN_EPS)
            * kvg_ref[...]).astype(BF16)
    ka_ref[0] = rope(_dot(latn, wk_ref[...]))[:, :HEAD_DIM].astype(BF16)
    vat_ref[0] = _dot_nt(wvt_ref[...], latn).astype(BF16)


def _rope_tables(s):
    half = HEAD_DIM // 2
    inv = ROPE_THETA ** (-jnp.arange(half, dtype=F32) / half)
    ang = jnp.arange(s).astype(F32)[:, None] * inv[None, :]
    cos = jnp.cos(ang)
    sin = jnp.sin(ang)
    reps = LANES // HEAD_DIM
    cos_l = jnp.tile(jnp.concatenate([cos, cos], axis=-1), (1, reps))
    sin_l = jnp.tile(jnp.concatenate([-sin, sin], axis=-1), (1, reps))
    return cos_l, sin_l, cos.T, sin.T


def _mixin(x, mod, w_mix_in, kv_norm_g, w_kv_up):
    bsz, s, d = x.shape
    tm = TOKEN_TILE
    split = np.cumsum([A_WIDTH, A_KV_RANK, A_IDX_HEADS * A_IDX_DIM, A_IDX_DIM, A_IDX_HEADS,
                       B_WIDTH, 6 * B_KV_WIDTH, 3 * B_HEADS]).tolist()
    w_qa, w_lat, w_qi, w_ki, w_wi, w_qb, w_kv6, w_gb, _ = jnp.split(w_mix_in, split, axis=-1)
    w_kc, w_vc, w_ks, w_vs, w_kw, w_vw = jnp.split(w_kv6, 6, axis=-1)
    zeros = lambda n: jnp.zeros((d, n), w_mix_in.dtype)
    w_std = jnp.concatenate([w_kc, w_vc, w_ks, w_kw, w_lat, w_ki, zeros(LANES - A_IDX_DIM)],
                            axis=-1).astype(BF16)
    w_t = jnp.concatenate([w_qa, w_qb, w_qi, w_vs, w_vw, w_gb, w_wi,
                           zeros(_MISC_ROWS - 3 * B_HEADS - A_IDX_HEADS)], axis=-1).T.astype(BF16)
    assert w_std.shape[1] == _P_STD and w_t.shape[0] == _P_T
    cos_l, sin_l, cos_t, sin_t = _rope_tables(s)
    w_k = jnp.concatenate([w_kv_up[:, :HEAD_DIM], jnp.zeros((A_KV_RANK, LANES - HEAD_DIM), F32)],
                          axis=-1).astype(BF16)
    w_vt = w_kv_up[:, HEAD_DIM:].T.astype(BF16)
    c2 = lambda b, i: (0, 0)
    feat = lambda rows: pl.BlockSpec((1, rows, tm), lambda b, i: (b, 0, i))
    tok = lambda width: pl.BlockSpec((1, tm, width), lambda b, i: (b, i, 0))
    grp = pl.BlockSpec((1, B_GROUPS, tm, HEAD_DIM), lambda b, i: (b, 0, i, 0))
    sds = jax.ShapeDtypeStruct
    out_shape = [sds((bsz, A_WIDTH, s), BF16),
                 sds((bsz, B_WIDTH, s), BF16),
                 sds((bsz, A_IDX_HEADS * A_IDX_DIM, s), BF16),
                 sds((bsz, HEAD_DIM, s), BF16),
                 sds((bsz, B_KV_WIDTH, s), BF16),
                 sds((bsz, B_KV_WIDTH, s), BF16),
                 sds((bsz, _MISC_ROWS, s), F32),
                 sds((bsz, s, HEAD_DIM), BF16),
                 sds((bsz, s, A_IDX_DIM), BF16),
                 sds((bsz, B_GROUPS, s, HEAD_DIM), BF16),
                 sds((bsz, B_GROUPS, s, HEAD_DIM), BF16),
                 sds((bsz, s, 2 * LANES), F32)]
    out_specs = [feat(A_WIDTH), feat(B_WIDTH), feat(A_IDX_HEADS * A_IDX_DIM), feat(HEAD_DIM),
                 feat(B_KV_WIDTH), feat(B_KV_WIDTH), feat(_MISC_ROWS),
                 tok(HEAD_DIM), tok(A_IDX_DIM), grp, grp, tok(2 * LANES)]
    return pl.pallas_call(
        _mixin_kernel,
        grid=(bsz, s // tm),
        in_specs=[pl.BlockSpec((1, tm, d), lambda b, i: (b, i, 0)),
                  pl.BlockSpec((1, N_MOD, d), lambda b, i: (b, 0, 0)),
                  pl.BlockSpec((d, _P_STD), c2),
                  pl.BlockSpec((_P_T, d), c2),
                  pl.BlockSpec((tm, LANES), lambda b, i: (i, 0)),
                  pl.BlockSpec((tm, LANES), lambda b, i: (i, 0)),
                  pl.BlockSpec((HEAD_DIM // 2, tm), lambda b, i: (0, i)),
                  pl.BlockSpec((HEAD_DIM // 2, tm), lambda b, i: (0, i)),
                  pl.BlockSpec((1, A_KV_RANK), c2),
                  pl.BlockSpec((A_KV_RANK, LANES), c2),
                  pl.BlockSpec((HEAD_DIM, A_KV_RANK), c2)],
        out_specs=out_specs,
        out_shape=out_shape,
        scratch_shapes=[pltpu.VMEM((tm, _P_STD), F32), pltpu.VMEM((_P_T, tm), F32)],
        compiler_params=_params(2),
        name="mixin",
    )(x, mod, w_std, w_t, cos_l, sin_l, cos_t, sin_t,
      kv_norm_g.reshape(1, A_KV_RANK), w_k, w_vt)


def _heads_on_lanes(qt, n_heads, dim):
    return jnp.concatenate([qt[h * dim:(h + 1) * dim, :] for h in range(n_heads)], axis=1)


def _fold_rows(x, op, tile=SUBLANES):
    while x.shape[0] % (2 * tile) == 0:
        h = x.shape[0] // 2
        x = op(x[:h], x[h:])
    return x


def _max_rows(x):
    return jnp.max(_fold_rows(x, jnp.maximum), axis=0, keepdims=True)


def _sum_rows(x):
    return jnp.sum(_fold_rows(x, jnp.add), axis=0, keepdims=True)


def _bit_transpose32(words):
    a = list(words)
    j, m = 16, 0x0000FFFF
    while j:
        k = 0
        while k < 32:
            t = (a[k] ^ lax.shift_right_logical(a[k + j], j)) & m
            a[k] = a[k] ^ t
            a[k + j] = a[k + j] ^ lax.shift_left(t, j)
            k = (k + j + 1) & ~j
        j >>= 1
        m ^= (m << j) & 0xFFFFFFFF
    return a


def _mask_cap(sel):
    return jnp.where(sel, jnp.inf, NEG)


def _with_ones_rows(v_t):
    return jnp.concatenate([v_t, jnp.ones((ONES_ROWS, v_t.shape[1]), v_t.dtype)], axis=0)


def _online_softmax_step_t(s_t, cap, v_ext, m_ref, acc_ref, n_heads):
    q = cap.shape[1]
    ps, alphas = [], []
    for h in range(n_heads):
        sl = slice(h * q, (h + 1) * q)
        sm = jnp.minimum(s_t[:, sl], cap)
        m_old = m_ref[:, sl]
        m_new = jnp.maximum(m_old, _max_rows(sm))
        ps.append(jnp.exp2(sm - m_new).astype(BF16))
        alphas.append(jnp.exp2(m_old - m_new))
        m_ref[:, sl] = m_new
    acc_ref[...] = (jnp.concatenate(alphas, axis=1) * acc_ref[...]
                    + _dot(v_ext, jnp.concatenate(ps, axis=1)))


def _pipelined_attention_t(n_chunks, scores_fn, cap_fn, v_fn, sa_ref, sb_ref, m_ref, acc_ref, n_heads):
    last = 2 * n_chunks - 1
    sa_ref[...] = scores_fn(0)

    def body(c, carry):
        i0 = 2 * c
        sb_ref[...] = scores_fn(i0 + 1)
        _online_softmax_step_t(sa_ref[...], cap_fn(i0), _with_ones_rows(v_fn(i0)), m_ref, acc_ref, n_heads)
        sa_ref[...] = scores_fn(jnp.minimum(i0 + 2, last))
        _online_softmax_step_t(sb_ref[...], cap_fn(i0 + 1), _with_ones_rows(v_fn(i0 + 1)),
                               m_ref, acc_ref, n_heads)
        return carry

    lax.fori_loop(0, n_chunks, body, 0)


def _finish_online_softmax_t(m_ref, acc_ref):
    d = acc_ref.shape[0] - ONES_ROWS
    return acc_ref[:d, :] * (jnp.where(m_ref[...] > NEG, 1.0, 0.0) / acc_ref[d:d + 1, :])


def _masked_softmax_t(s, cap):
    sm = jnp.minimum(s, cap)
    mx = _max_rows(sm)
    e = jnp.exp2(sm - mx)
    return e * (jnp.where(mx > NEG, 1.0, 0.0) / _sum_rows(e))


def _dsa_kernel(qat_ref, qit_ref, misct_ref, ki_ref, ka_ref, vat_ref, o_ref,
                key_ref, plane_ref, m_ref, acc_ref, *, topk, idx_bits):
    qb = Q_BLOCK
    kc = KEY_CHUNK
    q0 = pl.program_id(1) * qb
    n_chunks = lax.div(q0 + qb + kc - 1, kc)
    tq = q0 + lax.broadcasted_iota(I32, (1, qb), 1)
    row_k = lax.broadcasted_iota(I32, (kc, qb), 0)
    wi = misct_ref[0, _MISC_WI:_MISC_WI + A_IDX_HEADS, :]
    qi_t = qit_ref[0]
    qi_pairs = [_heads_on_lanes(qi_t[p * 2 * A_IDX_DIM:(p + 1) * 2 * A_IDX_DIM], 2, A_IDX_DIM)
                for p in range(A_IDX_HEADS // 2)]

    @pl.when((pl.program_id(0) == 0) & (pl.program_id(1) == 0))
    def _():
        plane_ref[...] = jnp.zeros_like(plane_ref)

    def score_chunk(c, carry):
        k0 = pl.multiple_of(c * kc, kc)
        ki = ki_ref[0, pl.ds(k0, kc), :]
        acc = jnp.zeros((kc, qb), F32)
        for p in range(A_IDX_HEADS // 2):
            lg = _dot(ki, qi_pairs[p])
            for j in range(2):
                h = 2 * p + j
                acc = acc + wi[h:h + 1, :] * jnp.maximum(lg[:, j * qb:(j + 1) * qb], 0.0)
        sc = jnp.where(k0 + row_k <= tq, acc, NEG)
        sc = jnp.where(sc == 0.0, 0.0, sc)
        bits = pltpu.bitcast(sc, I32)
        key = jnp.where(bits < 0, bits ^ 0x7FFFFFFF, bits)
        key_ref[pl.ds(k0, kc), :] = key
        ukey = key ^ INT_MIN
        group = PLANE_BITS * SUBLANES
        for g in range(kc // group):
            rows = [ukey[g * group + v * SUBLANES:g * group + (v + 1) * SUBLANES, :]
                    for v in range(PLANE_BITS)]
            base = pl.multiple_of(c * (kc // PLANE_BITS) + g * SUBLANES, SUBLANES)
            for b, plane in enumerate(_bit_transpose32(rows)):
                plane_ref[b, pl.ds(base, SUBLANES), :] = plane
        return carry

    lax.fori_loop(0, n_chunks, score_chunk, 0)

    wide = 2 * kc
    n_wide = lax.div(q0 + qb + wide - 1, wide)
    row_w = lax.broadcasted_iota(I32, (wide, qb), 0)

    @pl.when(n_wide * 2 > n_chunks)
    def _():
        key_ref[pl.ds(pl.multiple_of(n_chunks * kc, kc), kc), :] = jnp.full((kc, qb), INT_MIN, I32)

    def count(pred):
        def body(c, acc):
            k0 = pl.multiple_of(c * wide, wide)
            hit = jnp.where(pred(key_ref[pl.ds(k0, wide), :], k0 + row_w), 1.0, 0.0)
            return acc + _fold_rows(hit, jnp.add)
        acc = lax.fori_loop(0, n_wide, body, jnp.zeros((SUBLANES, qb), F32))
        return jnp.sum(acc, axis=0, keepdims=True)

    n_words = n_chunks * (kc // PLANE_BITS)
    word_row = lax.broadcasted_iota(I32, (plane_ref.shape[1], qb), 0)

    def search(it, carry):
        gt, eq, thr_u, cnt_thr = carry
        hit = eq & plane_ref[it]
        ge = gt | hit
        ones = _fold_rows(lax.population_count(ge), jnp.add).astype(F32)
        cnt = jnp.sum(ones, axis=0, keepdims=True)
        ok = cnt >= topk
        bit = lax.shift_left(jnp.int32(1), PLANE_BITS - 1 - it)
        return (jnp.where(ok, gt, ge), jnp.where(ok, hit, eq ^ hit),
                jnp.where(ok, thr_u | bit, thr_u), jnp.where(ok, cnt, cnt_thr))

    eq0 = jnp.where(word_row < n_words, jnp.int32(-1), jnp.int32(0))
    cnt0 = jnp.full((1, qb), 1.0, F32) * (n_chunks * kc).astype(F32)
    _, _, thr_u, cnt_thr = lax.fori_loop(
        0, PLANE_BITS, search, (jnp.zeros_like(eq0), eq0, jnp.zeros((1, qb), I32), cnt0))
    thr = thr_u ^ INT_MIN

    has_tie = jnp.max(jnp.where((cnt_thr > topk) & (thr > NEG_KEY), 1.0, 0.0)) > 0.0

    def tie_search():
        need = topk - count(lambda key, pos: key > thr)

        def step(it, last):
            cand = last + lax.shift_left(jnp.int32(1), idx_bits - 1 - it)
            below = count(lambda key, pos: (key == thr) & (pos < cand))
            return jnp.where(below < need, cand, last)

        return lax.fori_loop(0, idx_bits, step, jnp.zeros((1, qb), I32))

    last_eq = lax.cond(has_tie, tie_search, lambda: jnp.full((1, qb), INT_MAX, I32))

    m_ref[...] = jnp.full_like(m_ref, NEG)
    acc_ref[...] = jnp.zeros_like(acc_ref)
    q_all = _heads_on_lanes(qat_ref[0], A_HEADS, HEAD_DIM)

    def attend(c, carry):
        k0 = pl.multiple_of(c * kc, kc)
        key = key_ref[pl.ds(k0, kc), :]
        pos = k0 + row_k
        sel = ((key > thr) | ((key == thr) & (pos <= last_eq))) & (pos <= tq)
        s_t = _dot(ka_ref[0, pl.ds(k0, kc), :], q_all)
        _online_softmax_step_t(s_t, _mask_cap(sel), _with_ones_rows(vat_ref[0, :, pl.ds(k0, kc)]),
                               m_ref, acc_ref, A_HEADS)
        return carry

    lax.fori_loop(0, n_chunks, attend, 0)
    o_t = _finish_online_softmax_t(m_ref, acc_ref)
    o_t = jnp.concatenate([o_t[:, h * qb:(h + 1) * qb] for h in range(A_HEADS)], axis=0)
    o_ref[0] = o_t.T.astype(o_ref.dtype)


def _dsa(qa_t, qi_t, misc_t, ki, ka, va_t):
    bsz, _, s = qa_t.shape
    qb = Q_BLOCK
    assert s % (2 * KEY_CHUNK) == 0 and KEY_CHUNK >= A_TOPK_MAX
    topk = min(A_TOPK_MAX, s // 4)
    idx_bits = int(np.ceil(np.log2(s)))
    feat = lambda rows: pl.BlockSpec((1, rows, qb), lambda b, n: (b, 0, n))
    return pl.pallas_call(
        functools.partial(_dsa_kernel, topk=topk, idx_bits=idx_bits),
        grid=(bsz, s // qb),
        in_specs=[feat(A_WIDTH), feat(A_IDX_HEADS * A_IDX_DIM), feat(_MISC_ROWS),
                  pl.BlockSpec((1, s, A_IDX_DIM), lambda b, n: (b, 0, 0)),
                  pl.BlockSpec((1, s, HEAD_DIM), lambda b, n: (b, 0, 0)),
                  pl.BlockSpec((1, HEAD_DIM, s), lambda b, n: (b, 0, 0))],
        out_specs=pl.BlockSpec((1, qb, A_WIDTH), lambda b, n: (b, n, 0)),
        out_shape=jax.ShapeDtypeStruct((bsz, s, A_WIDTH), BF16),
        scratch_shapes=[pltpu.VMEM((s, qb), I32),
                        pltpu.VMEM((PLANE_BITS, s // PLANE_BITS, qb), I32),
                        pltpu.VMEM((1, A_HEADS * qb), F32),
                        pltpu.VMEM((HEAD_DIM + ONES_ROWS, A_HEADS * qb), F32)],
        compiler_params=_params(2),
        name="dsa",
    )(qa_t, qi_t, misc_t, ki, ka, va_t)


def _compress_kernel(xk_ref, xv_ref, pek_ref, pev_ref, w1k_ref, w1v_ref, w2k_ref, w2vt_ref,
                     ok_ref, ovt_ref):
    def hidden(x_ref, pe_ref, w1_ref):
        x = x_ref[0, 0]
        n = x.shape[0]
        nxt = pltpu.roll(x, n - 1, 0)
        a = (x + pe_ref[0:1, :]).astype(BF16)
        b = (nxt + pe_ref[1:2, :]).astype(BF16)
        hid = _dot(a, w1_ref[0]) + _dot(b, w1_ref[1])
        return (hid * jax.nn.sigmoid(hid)).astype(BF16)

    ok_ref[0, 0] = _dot(hidden(xk_ref, pek_ref, w1k_ref), w2k_ref[...]).astype(ok_ref.dtype)
    ovt_ref[0, 0] = _dot_nt(w2vt_ref[...], hidden(xv_ref, pev_ref, w1v_ref)).astype(ovt_ref.dtype)


def _compress(cmp_in, pe_k, w1_k, w2_k, pe_v, w1_v, w2_v):
    bsz, s, _ = cmp_in.shape
    nch = s // B_CMP_STRIDE
    cw = B_CMP_STRIDE * HEAD_DIM

    def chunks(t):
        t = t.reshape(bsz, nch, B_CMP_STRIDE, B_GROUPS, HEAD_DIM)
        return t.transpose(0, 3, 1, 2, 4).reshape(bsz, B_GROUPS, nch, cw)

    xk = chunks(cmp_in[:, :, :LANES])
    xv = chunks(cmp_in[:, :, LANES:])
    xspec = pl.BlockSpec((1, 1, nch, cw), lambda b, g: (b, g, 0, 0))
    c2 = lambda b, g: (0, 0)
    c3 = lambda b, g: (0, 0, 0)
    return pl.pallas_call(
        _compress_kernel,
        grid=(bsz, B_GROUPS),
        in_specs=[xspec, xspec,
                  pl.BlockSpec((2, cw), c2), pl.BlockSpec((2, cw), c2),
                  pl.BlockSpec((2, cw, B_CMP_HIDDEN), c3), pl.BlockSpec((2, cw, B_CMP_HIDDEN), c3),
                  pl.BlockSpec((B_CMP_HIDDEN, HEAD_DIM), c2), pl.BlockSpec((HEAD_DIM, B_CMP_HIDDEN), c2)],
        out_specs=[pl.BlockSpec((1, 1, nch, HEAD_DIM), lambda b, g: (b, g, 0, 0)),
                   pl.BlockSpec((1, 1, HEAD_DIM, nch), lambda b, g: (b, g, 0, 0))],
        out_shape=[jax.ShapeDtypeStruct((bsz, B_GROUPS, nch, HEAD_DIM), BF16),
                   jax.ShapeDtypeStruct((bsz, B_GROUPS, HEAD_DIM, nch), BF16)],
        compiler_params=_params(2),
        name="compress",
    )(xk, xv, pe_k.reshape(2, cw), pe_v.reshape(2, cw),
      w1_k.astype(BF16).reshape(2, cw, B_CMP_HIDDEN), w1_v.astype(BF16).reshape(2, cw, B_CMP_HIDDEN),
      w2_k.astype(BF16), w2_v.T.astype(BF16))


def _nsa_kernel(qbt_ref, misct_ref, kcmp_ref, vcmpt_ref, ks_ref, vst_ref, kw_ref, vwt_ref, ovlt_ref,
                o_ref, chosen_ref, sa_ref, sb_ref, m_ref, acc_ref, out_ref, *, n_slc, top_n):
    qb = Q_BLOCK
    kc = KEY_CHUNK
    rep = B_REP
    slc_shift = int(np.log2(B_SLC_LEN))
    q0 = pl.program_id(1) * qb
    n_chunks = lax.div(q0 + qb + kc - 1, kc)
    tq = q0 + lax.broadcasted_iota(I32, (1, qb), 1)
    gates = misct_ref[0]
    n_cmp = kcmp_ref.shape[2]
    n_rows = chosen_ref.shape[0]
    cmp_end = lax.broadcasted_iota(I32, (n_cmp, qb), 0) * B_CMP_STRIDE + (B_CMP_LEN - 1)
    cap_c = _mask_cap(cmp_end <= tq)
    jj = lax.broadcasted_iota(I32, (n_rows, qb), 0)
    jj_f = jj.astype(F32)
    blk_q = lax.shift_right_logical(tq, slc_shift)
    forced = (jj == 0) | (jj == blk_q) | (jj == blk_q - 1)
    half = kc // 2
    row_h = lax.broadcasted_iota(I32, (half, qb), 0)
    w0 = pl.multiple_of(jnp.maximum(q0 - B_WINDOW, 0), qb)
    wlen = B_WINDOW + qb
    pos_w = w0 + lax.broadcasted_iota(I32, (wlen, qb), 0)
    cap_w = _mask_cap((pos_w <= tq) & (pos_w > tq - B_WINDOW))

    for g in range(B_GROUPS):
        q4 = _heads_on_lanes(qbt_ref[0, g * rep * HEAD_DIM:(g + 1) * rep * HEAD_DIM, :], rep, HEAD_DIM)
        vrows = slice(g * HEAD_DIM, (g + 1) * HEAD_DIM)

        s_c = _dot(kcmp_ref[0, g], q4)
        p_c = [_masked_softmax_t(s_c[:, r * qb:(r + 1) * qb], cap_c) for r in range(rep)]
        o_c = _dot(vcmpt_ref[0, g], jnp.concatenate([p.astype(BF16) for p in p_c], axis=1))

        p_sum = p_c[0]
        for r in range(1, rep):
            p_sum = p_sum + p_c[r]
        hi = p_sum.astype(BF16)
        r1 = p_sum - hi.astype(F32)
        mid = r1.astype(BF16)
        lo = (r1 - mid.astype(F32)).astype(BF16)
        ovl = ovlt_ref[...]
        imp = _dot(ovl, hi) + _dot(ovl, mid) + _dot(ovl, lo)
        val = jnp.where(jj <= blk_q, imp + FORCE_BONUS * jnp.where(forced, 1.0, 0.0), NEG)
        val = jnp.where(jj < n_slc, val, -jnp.inf)

        def pick(i, carry):
            cur, chosen = carry
            mx = jnp.max(cur, axis=0, keepdims=True)
            first = jnp.min(jnp.where(cur == mx, jj_f, float(n_rows)), axis=0, keepdims=True)
            hit = jj_f == first
            return jnp.where(hit, -jnp.inf, cur), jnp.where(hit, 1.0, chosen)

        _, chosen = lax.fori_loop(0, top_n, pick, (val, jnp.zeros((n_rows, qb), F32)))
        chosen_ref[...] = chosen

        m_ref[...] = jnp.full_like(m_ref, NEG)
        acc_ref[...] = jnp.zeros_like(acc_ref)

        def keys_at(i):
            return pl.ds(pl.multiple_of(i * half, half), half)

        def cap_fn(i):
            j0 = i * (half // B_SLC_LEN)
            blocks = [jnp.broadcast_to(chosen_ref[pl.ds(j0 + b, 1), :], (B_SLC_LEN, qb))
                      for b in range(half // B_SLC_LEN)]
            return _mask_cap((jnp.concatenate(blocks, axis=0) > 0.5) & (i * half + row_h <= tq))

        _pipelined_attention_t(n_chunks,
                               lambda i: _dot(ks_ref[0, g, keys_at(i), :], q4),
                               cap_fn,
                               lambda i: vst_ref[0, vrows, keys_at(i)],
                               sa_ref, sb_ref, m_ref, acc_ref, rep)
        o_s = _finish_online_softmax_t(m_ref, acc_ref)

        s_w = _dot(kw_ref[0, g, pl.ds(w0, wlen), :], q4)
        p_w = [_masked_softmax_t(s_w[:, r * qb:(r + 1) * qb], cap_w).astype(BF16) for r in range(rep)]
        o_w = _dot(vwt_ref[0, vrows, pl.ds(w0, wlen)], jnp.concatenate(p_w, axis=1))

        for r in range(rep):
            h = g * rep + r
            sl = slice(r * qb, (r + 1) * qb)
            out_ref[h * HEAD_DIM:(h + 1) * HEAD_DIM, :] = (
                gates[h:h + 1, :] * o_c[:, sl] + gates[B_HEADS + h:B_HEADS + h + 1, :] * o_s[:, sl]
                + gates[2 * B_HEADS + h:2 * B_HEADS + h + 1, :] * o_w[:, sl])

    o_ref[0] = out_ref[...].T.astype(o_ref.dtype)


def _nsa(qb_t, misc_t, kcmp, vcmp_t, ks, vs_t, kw, vw_t):
    bsz, _, s = qb_t.shape
    qblk = Q_BLOCK
    n_cmp = kcmp.shape[2]
    n_slc = s // B_SLC_LEN
    top_n = min(B_SLC_TOP, n_slc)
    n_rows = -(-n_slc // SUBLANES) * SUBLANES
    assert s % KEY_CHUNK == 0 and s >= B_WINDOW + qblk and KEY_CHUNK % B_SLC_LEN == 0
    slc_start = np.arange(n_rows)[:, None] * B_SLC_LEN
    cmp_start = np.arange(n_cmp)[None, :] * B_CMP_STRIDE
    overlap_t = ((cmp_start < slc_start + B_SLC_LEN) & (slc_start < cmp_start + B_CMP_LEN)
                 & (np.arange(n_rows)[:, None] < n_slc))
    overlap_t = jnp.asarray(overlap_t, BF16)
    feat = lambda rows: pl.BlockSpec((1, rows, qblk), lambda b, n: (b, 0, n))
    k_spec = pl.BlockSpec((1, B_GROUPS, s, HEAD_DIM), lambda b, n: (b, 0, 0, 0))
    vt_spec = pl.BlockSpec((1, B_KV_WIDTH, s), lambda b, n: (b, 0, 0))
    return pl.pallas_call(
        functools.partial(_nsa_kernel, n_slc=n_slc, top_n=top_n),
        grid=(bsz, s // qblk),
        in_specs=[feat(B_WIDTH), feat(_MISC_ROWS),
                  pl.BlockSpec((1, B_GROUPS, n_cmp, HEAD_DIM), lambda b, n: (b, 0, 0, 0)),
                  pl.BlockSpec((1, B_GROUPS, HEAD_DIM, n_cmp), lambda b, n: (b, 0, 0, 0)),
                  k_spec, vt_spec, k_spec, vt_spec,
                  pl.BlockSpec((n_rows, n_cmp), lambda b, n: (0, 0))],
        out_specs=pl.BlockSpec((1, qblk, B_WIDTH), lambda b, n: (b, n, 0)),
        out_shape=jax.ShapeDtypeStruct((bsz, s, B_WIDTH), BF16),
        scratch_shapes=[pltpu.VMEM((n_rows, qblk), F32),
                        pltpu.VMEM((KEY_CHUNK // 2, B_REP * qblk), F32),
                        pltpu.VMEM((KEY_CHUNK // 2, B_REP * qblk), F32),
                        pltpu.VMEM((1, B_REP * qblk), F32),
                        pltpu.VMEM((HEAD_DIM + ONES_ROWS, B_REP * qblk), F32),
                        pltpu.VMEM((B_WIDTH, qblk), F32)],
        compiler_params=_params(2),
        name="nsa",
    )(qb_t, misc_t, kcmp, vcmp_t, ks, vs_t, kw, vw_t, overlap_t)


def _merge_kernel(x_ref, mod_ref, ya_ref, yb_ref, wgm_ref, wa_ref, wb_ref, wo_ref, g_ref, b_ref,
                  o_ref, *, alpha):
    x = x_ref[0]
    d = x.shape[-1]
    sh = mod_ref[0, 3:4, :]
    sc = mod_ref[0, 4:5, :]
    gt = mod_ref[0, 5:6, :]
    u = (x * (1.0 + sc) + sh).astype(BF16)
    gm = jax.nn.sigmoid(_dot(u, wgm_ref[...]))
    merged = gm[:, :d] * _dot(ya_ref[0], wa_ref[...]) + gm[:, d:] * _dot(yb_ref[0], wb_ref[...])
    mix = _dot(merged.astype(BF16), wo_ref[...])
    o_ref[0] = _layer_norm(alpha * x + gt * mix, g_ref[...], b_ref[...])


def _merge(x, mod, ya, yb, w_gm, w_br_a, w_br_b, w_mix_out, ln_g, ln_b, alpha):
    bsz, s, d = x.shape
    tm = TOKEN_TILE
    c2 = lambda b, i: (0, 0)
    tok = lambda width: pl.BlockSpec((1, tm, width), lambda b, i: (b, i, 0))
    return pl.pallas_call(
        functools.partial(_merge_kernel, alpha=alpha),
        grid=(bsz, s // tm),
        in_specs=[tok(d),
                  pl.BlockSpec((1, N_MOD, d), lambda b, i: (b, 0, 0)),
                  tok(A_WIDTH), tok(B_WIDTH),
                  pl.BlockSpec((d, 2 * d), c2),
                  pl.BlockSpec((A_WIDTH, d), c2),
                  pl.BlockSpec((B_WIDTH, d), c2),
                  pl.BlockSpec((d, d), c2),
                  pl.BlockSpec((1, d), c2), pl.BlockSpec((1, d), c2)],
        out_specs=tok(d),
        out_shape=jax.ShapeDtypeStruct((bsz, s, d), F32),
        compiler_params=_params(2),
        name="merge",
    )(x, mod, ya, yb, w_gm.astype(BF16), w_br_a.astype(BF16), w_br_b.astype(BF16),
      w_mix_out.astype(BF16), ln_g.reshape(1, d), ln_b.reshape(1, d))


def kernel(x, c, w_ada, b_ada, ffn1_w_in, ffn1_w_out, ln1_g, ln1_b, w_mix_in, kv_norm_g, w_kv_up,
           cmp_pe_k, cmp_w1_k, cmp_w2_k, cmp_pe_v, cmp_w1_v, cmp_w2_v, w_br_a, w_br_b, w_mix_out,
           ln2_g, ln2_b, ffn2_w_in, ffn2_w_out, ln3_g, ln3_b):
    bsz, s, d = x.shape
    depth = w_ada.shape[0]
    alpha = (2.0 * depth) ** 0.25
    assert s % TOKEN_TILE == 0 and s % KEY_CHUNK == 0
    for l in range(depth):
        mod = _ada(c, w_ada[l], b_ada[l]).reshape(bsz, N_MOD, d)
        x = _ffn(x, mod, ffn1_w_in[l], ffn1_w_out[l], ln1_g[l], ln1_b[l], 0, alpha)
        qa_t, qb_t, qi_t, va_t, vs_t, vw_t, misc_t, ka, ki, ks, kw, cmp_in = _mixin(
            x, mod, w_mix_in[l], kv_norm_g[l], w_kv_up[l])
        ya = _dsa(qa_t, qi_t, misc_t, ki, ka, va_t)
        kcmp, vcmp_t = _compress(cmp_in, cmp_pe_k[l], cmp_w1_k[l], cmp_w2_k[l],
                                 cmp_pe_v[l], cmp_w1_v[l], cmp_w2_v[l])
        yb = _nsa(qb_t, misc_t, kcmp, vcmp_t, ks, vs_t, kw, vw_t)
        w_gm = w_mix_in[l][:, -2 * d:]
        x = _merge(x, mod, ya, yb, w_gm, w_br_a[l], w_br_b[l], w_mix_out[l], ln2_g[l], ln2_b[l], alpha)
        x = _ffn(x, mod, ffn2_w_in[l], ffn2_w_out[l], ln3_g[l], ln3_b[l], 6, alpha)
    return x
```

```python
import functools

import numpy as np
import jax
import jax.numpy as jnp
from jax import lax
from jax.experimental import pallas as pl
from jax.experimental.pallas import tpu as pltpu

F32 = jnp.float32
BF16 = jnp.bfloat16
I32 = jnp.int32

HEAD_DIM = 64
ROPE_THETA = 10000.0
A_HEADS = 8
A_WIDTH = A_HEADS * HEAD_DIM
A_KV_RANK = 128
A_IDX_HEADS = 4
A_IDX_DIM = 64
A_TOPK_MAX = 256
B_HEADS = 8
B_GROUPS = 2
B_REP = B_HEADS // B_GROUPS
B_WIDTH = B_HEADS * HEAD_DIM
B_KV_WIDTH = B_GROUPS * HEAD_DIM
B_CMP_STRIDE = 16
B_CMP_LEN = 2 * B_CMP_STRIDE
B_CMP_HIDDEN = 256
B_SLC_LEN = 64
B_SLC_TOP = 16
B_WINDOW = 512
N_MOD = 9
LN_EPS = 1e-5
NEG = -1e30
FORCE_BONUS = 1e4

LANES = 128
SUBLANES = 8
Q_BLOCK = LANES
KEY_CHUNK = 512
TOKEN_TILE = 512
FF_CHUNK = 256
PLANE_BITS = 32
ONES_ROWS = 16
VMEM_LIMIT = 56 * 1024 * 1024

NEG_KEY = int(np.array(NEG, np.float32).view(np.int32)) ^ 0x7FFFFFFF
INT_MIN = -(2 ** 31)
INT_MAX = 2 ** 31 - 1


def _dot(a, b):
    return jnp.dot(a, b, preferred_element_type=F32)


def _dot_nt(a, b):
    return lax.dot_general(a, b, (((1,), (1,)), ((), ())), preferred_element_type=F32)


def _layer_norm(y, g, b):
    mu = jnp.mean(y, axis=-1, keepdims=True)
    d = y - mu
    var = jnp.mean(d * d, axis=-1, keepdims=True)
    return d * lax.rsqrt(var + LN_EPS) * g + b


def _params(n_axes):
    return pltpu.CompilerParams(dimension_semantics=("arbitrary",) * n_axes,
                                vmem_limit_bytes=VMEM_LIMIT)


def _ada_kernel(c_ref, w_ref, b_ref, o_ref):
    c = c_ref[...]
    ca = c * jax.nn.sigmoid(c)
    o_ref[...] = jnp.dot(ca, w_ref[...], preferred_element_type=F32,
                         precision=lax.Precision.HIGHEST) + b_ref[...]


def _ada(c, w_ada, b_ada):
    bsz, d = c.shape
    n = w_ada.shape[1]
    tn = n // 8
    return pl.pallas_call(
        _ada_kernel,
        grid=(n // tn,),
        in_specs=[pl.BlockSpec((bsz, d), lambda j: (0, 0)),
                  pl.BlockSpec((d, tn), lambda j: (0, j)),
                  pl.BlockSpec((1, tn), lambda j: (0, j))],
        out_specs=pl.BlockSpec((bsz, tn), lambda j: (0, j)),
        out_shape=jax.ShapeDtypeStruct((bsz, n), F32),
        compiler_params=_params(1),
        name="ada",
    )(c, w_ada, b_ada.reshape(1, n))


def _ffn_kernel(x_ref, mod_ref, wi_ref, wo_ref, g_ref, b_ref, o_ref, u_ref, acc_ref,
                *, mod_base, n_chunks, alpha):
    x = x_ref[0]
    sh = mod_ref[0, mod_base:mod_base + 1, :]
    sc = mod_ref[0, mod_base + 1:mod_base + 2, :]
    gt = mod_ref[0, mod_base + 2:mod_base + 3, :]
    u_ref[...] = (x * (1.0 + sc) + sh).astype(BF16)
    d_ff = wo_ref.shape[0]
    for j in range(n_chunks):
        cols = slice(j * FF_CHUNK, (j + 1) * FF_CHUNK)
        u = u_ref[...]
        a = _dot(u, wi_ref[:, cols])
        b = _dot(u, wi_ref[:, d_ff + j * FF_CHUNK:d_ff + (j + 1) * FF_CHUNK])
        h = (a * jax.nn.sigmoid(a) * b).astype(BF16)
        d = _dot(h, wo_ref[cols, :])
        if j == 0:
            acc_ref[...] = d
        else:
            acc_ref[...] += d
    y = alpha * x + (0.5 * gt) * acc_ref[...]
    o_ref[0] = _layer_norm(y, g_ref[...], b_ref[...])


def _ffn(x, mod, w_in, w_out, ln_g, ln_b, mod_base, alpha):
    bsz, s, d = x.shape
    d_ff = w_out.shape[0]
    n_chunks = d_ff // FF_CHUNK
    tm = TOKEN_TILE
    assert d_ff % FF_CHUNK == 0 and d_ff % LANES == 0
    const2 = lambda b, i: (0, 0)
    return pl.pallas_call(
        functools.partial(_ffn_kernel, mod_base=mod_base, n_chunks=n_chunks, alpha=alpha),
        grid=(bsz, s // tm),
        in_specs=[pl.BlockSpec((1, tm, d), lambda b, i: (b, i, 0)),
                  pl.BlockSpec((1, N_MOD, d), lambda b, i: (b, 0, 0)),
                  pl.BlockSpec((d, 2 * d_ff), const2),
                  pl.BlockSpec((d_ff, d), const2),
                  pl.BlockSpec((1, d), lambda b, i: (0, 0)),
                  pl.BlockSpec((1, d), lambda b, i: (0, 0))],
        out_specs=pl.BlockSpec((1, tm, d), lambda b, i: (b, i, 0)),
        out_shape=jax.ShapeDtypeStruct((bsz, s, d), F32),
        scratch_shapes=[pltpu.VMEM((tm, d), BF16), pltpu.VMEM((tm, d), F32)],
        compiler_params=_params(2),
        name="ffn",
    )(x, mod, w_in.astype(BF16), w_out.astype(BF16), ln_g.reshape(1, d), ln_b.reshape(1, d))


_S_KC = 0
_S_VC = _S_KC + B_KV_WIDTH
_S_KS = _S_VC + B_KV_WIDTH
_S_KW = _S_KS + B_KV_WIDTH
_S_LAT = _S_KW + B_KV_WIDTH
_S_KI = _S_LAT + A_KV_RANK
_P_STD = _S_KI + LANES
_T_QA = 0
_T_QB = _T_QA + A_WIDTH
_T_QI = _T_QB + B_WIDTH
_T_VS = _T_QI + A_IDX_HEADS * A_IDX_DIM
_T_VW = _T_VS + B_KV_WIDTH
_T_MISC = _T_VW + B_KV_WIDTH
_MISC_ROWS = 32
_MISC_WI = 3 * B_HEADS
_P_T = _T_MISC + _MISC_ROWS


def _mixin_kernel(x_ref, mod_ref, ws_ref, wt_ref, cos_ref, sin_ref, cost_ref, sint_ref,
                  kvg_ref, wk_ref, wvt_ref,
                  qat_ref, qbt_ref, qit_ref, vat_ref, vst_ref, vwt_ref, misct_ref,
                  ka_ref, ki_ref, ks_ref, kw_ref, cmp_ref, hs_ref, ht_ref):
    x = x_ref[0]
    sh = mod_ref[0, 3:4, :]
    sc = mod_ref[0, 4:5, :]
    u = (x * (1.0 + sc) + sh).astype(BF16)
    hs_ref[...] = _dot(u, ws_ref[...])
    ht_ref[...] = _dot_nt(wt_ref[...], u)

    tm = x.shape[0]
    half = HEAD_DIM // 2
    scale = HEAD_DIM ** -0.5 * float(np.log2(np.e))

    cos_t = cost_ref[...]
    sin_t = sint_ref[...]

    def rope_t(src0, dst_ref, dst0, mul):
        x1 = ht_ref[src0:src0 + half, :]
        x2 = ht_ref[src0 + half:src0 + HEAD_DIM, :]
        o1 = x1 * cos_t - x2 * sin_t
        o2 = x1 * sin_t + x2 * cos_t
        if mul != 1.0:
            o1 = o1 * mul
            o2 = o2 * mul
        dst_ref[0, dst0:dst0 + half, :] = o1.astype(dst_ref.dtype)
        dst_ref[0, dst0 + half:dst0 + HEAD_DIM, :] = o2.astype(dst_ref.dtype)

    for h in range(A_HEADS):
        rope_t(_T_QA + h * HEAD_DIM, qat_ref, h * HEAD_DIM, scale)
    for h in range(B_HEADS):
        rope_t(_T_QB + h * HEAD_DIM, qbt_ref, h * HEAD_DIM, scale)
    for h in range(A_IDX_HEADS):
        rope_t(_T_QI + h * A_IDX_DIM, qit_ref, h * A_IDX_DIM, 1.0)
    vst_ref[0] = ht_ref[_T_VS:_T_VS + B_KV_WIDTH, :].astype(BF16)
    vwt_ref[0] = ht_ref[_T_VW:_T_VW + B_KV_WIDTH, :].astype(BF16)
    misc = ht_ref[_T_MISC:_T_MISC + _MISC_ROWS, :]
    row = lax.broadcasted_iota(I32, misc.shape, 0)
    misct_ref[0] = jnp.where(row < _MISC_WI, jax.nn.sigmoid(misc), misc * (A_IDX_HEADS ** -0.5))

    cos = cos_ref[...]
    sin = sin_ref[...]
    lane = lax.broadcasted_iota(I32, (tm, LANES), 1)
    first_half = (lane & (HEAD_DIM - 1)) < half

    def rope(t):
        rot = jnp.where(first_half, pltpu.roll(t, LANES - half, 1), pltpu.roll(t, half, 1))
        return t * cos + rot * sin

    def chunk(c0):
        return hs_ref[:, c0:c0 + LANES]

    cmp_ref[0, :, :LANES] = rope(chunk(_S_KC))
    cmp_ref[0, :, LANES:] = chunk(_S_VC)
    ks = rope(chunk(_S_KS)).astype(BF16)
    kw = rope(chunk(_S_KW)).astype(BF16)
    for g in range(B_GROUPS):
        ks_ref[0, g] = ks[:, g * HEAD_DIM:(g + 1) * HEAD_DIM]
        kw_ref[0, g] = kw[:, g * HEAD_DIM:(g + 1) * HEAD_DIM]
    ki_ref[0] = rope(chunk(_S_KI))[:, :A_IDX_DIM].astype(BF16)

    lat = chunk(_S_LAT)
    latn = (lat * lax.rsqrt(jnp.mean(lat * lat, axis=-1, keepdims=True) + LN_EPS)
            * kvg_ref[...]).astype(BF16)
    ka_ref[0] = rope(_dot(latn, wk_ref[...]))[:, :HEAD_DIM].astype(BF16)
    vat_ref[0] = _dot_nt(wvt_ref[...], latn).astype(BF16)


def _rope_tables(s):
    half = HEAD_DIM // 2
    inv = ROPE_THETA ** (-jnp.arange(half, dtype=F32) / half)
    ang = jnp.arange(s).astype(F32)[:, None] * inv[None, :]
    cos = jnp.cos(ang)
    sin = jnp.sin(ang)
    reps = LANES // HEAD_DIM
    cos_l = jnp.tile(jnp.concatenate([cos, cos], axis=-1), (1, reps))
    sin_l = jnp.tile(jnp.concatenate([-sin, sin], axis=-1), (1, reps))
    return cos_l, sin_l, cos.T, sin.T


def _mixin(x, mod, w_mix_in, kv_norm_g, w_kv_up):
    bsz, s, d = x.shape
    tm = TOKEN_TILE
    split = np.cumsum([A_WIDTH, A_KV_RANK, A_IDX_HEADS * A_IDX_DIM, A_IDX_DIM, A_IDX_HEADS,
                       B_WIDTH, 6 * B_KV_WIDTH, 3 * B_HEADS]).tolist()
    w_qa, w_lat, w_qi, w_ki, w_wi, w_qb, w_kv6, w_gb, _ = jnp.split(w_mix_in, split, axis=-1)
    w_kc, w_vc, w_ks, w_vs, w_kw, w_vw = jnp.split(w_kv6, 6, axis=-1)
    zeros = lambda n: jnp.zeros((d, n), w_mix_in.dtype)
    w_std = jnp.concatenate([w_kc, w_vc, w_ks, w_kw, w_lat, w_ki, zeros(LANES - A_IDX_DIM)],
                            axis=-1).astype(BF16)
    w_t = jnp.concatenate([w_qa, w_qb, w_qi, w_vs, w_vw, w_gb, w_wi,
                           zeros(_MISC_ROWS - 3 * B_HEADS - A_IDX_HEADS)], axis=-1).T.astype(BF16)
    assert w_std.shape[1] == _P_STD and w_t.shape[0] == _P_T
    cos_l, sin_l, cos_t, sin_t = _rope_tables(s)
    w_k = jnp.concatenate([w_kv_up[:, :HEAD_DIM], jnp.zeros((A_KV_RANK, LANES - HEAD_DIM), F32)],
                          axis=-1).astype(BF16)
    w_vt = w_kv_up[:, HEAD_DIM:].T.astype(BF16)
    c2 = lambda b, i: (0, 0)
    feat = lambda rows: pl.BlockSpec((1, rows, tm), lambda b, i: (b, 0, i))
    tok = lambda width: pl.BlockSpec((1, tm, width), lambda b, i: (b, i, 0))
    grp = pl.BlockSpec((1, B_GROUPS, tm, HEAD_DIM), lambda b, i: (b, 0, i, 0))
    sds = jax.ShapeDtypeStruct
    out_shape = [sds((bsz, A_WIDTH, s), BF16),
                 sds((bsz, B_WIDTH, s), BF16),
                 sds((bsz, A_IDX_HEADS * A_IDX_DIM, s), BF16),
                 sds((bsz, HEAD_DIM, s), BF16),
                 sds((bsz, B_KV_WIDTH, s), BF16),
                 sds((bsz, B_KV_WIDTH, s), BF16),
                 sds((bsz, _MISC_ROWS, s), F32),
                 sds((bsz, s, HEAD_DIM), BF16),
                 sds((bsz, s, A_IDX_DIM), BF16),
                 sds((bsz, B_GROUPS, s, HEAD_DIM), BF16),
                 sds((bsz, B_GROUPS, s, HEAD_DIM), BF16),
                 sds((bsz, s, 2 * LANES), F32)]
    out_specs = [feat(A_WIDTH), feat(B_WIDTH), feat(A_IDX_HEADS * A_IDX_DIM), feat(HEAD_DIM),
                 feat(B_KV_WIDTH), feat(B_KV_WIDTH), feat(_MISC_ROWS),
                 tok(HEAD_DIM), tok(A_IDX_DIM), grp, grp, tok(2 * LANES)]
    return pl.pallas_call(
        _mixin_kernel,
        grid=(bsz, s // tm),
        in_specs=[pl.BlockSpec((1, tm, d), lambda b, i: (b, i, 0)),
                  pl.BlockSpec((1, N_MOD, d), lambda b, i: (b, 0, 0)),
                  pl.BlockSpec((d, _P_STD), c2),
                  pl.BlockSpec((_P_T, d), c2),
                  pl.BlockSpec((tm, LANES), lambda b, i: (i, 0)),
                  pl.BlockSpec((tm, LANES), lambda b, i: (i, 0)),
                  pl.BlockSpec((HEAD_DIM // 2, tm), lambda b, i: (0, i)),
                  pl.BlockSpec((HEAD_DIM // 2, tm), lambda b, i: (0, i)),
                  pl.BlockSpec((1, A_KV_RANK), c2),
                  pl.BlockSpec((A_KV_RANK, LANES), c2),
                  pl.BlockSpec((HEAD_DIM, A_KV_RANK), c2)],
        out_specs=out_specs,
        out_shape=out_shape,
        scratch_shapes=[pltpu.VMEM((tm, _P_STD), F32), pltpu.VMEM((_P_T, tm), F32)],
        compiler_params=_params(2),
        name="mixin",
    )(x, mod, w_std, w_t, cos_l, sin_l, cos_t, sin_t,
      kv_norm_g.reshape(1, A_KV_RANK), w_k, w_vt)


def _heads_on_lanes(qt, n_heads, dim):
    return jnp.concatenate([qt[h * dim:(h + 1) * dim, :] for h in range(n_heads)], axis=1)


def _fold_rows(x, op, tile=SUBLANES):
    while x.shape[0] % (2 * tile) == 0:
        h = x.shape[0] // 2
        x = op(x[:h], x[h:])
    return x


def _max_rows(x):
    return jnp.max(_fold_rows(x, jnp.maximum), axis=0, keepdims=True)


def _sum_rows(x):
    return jnp.sum(_fold_rows(x, jnp.add), axis=0, keepdims=True)


def _bit_transpose32(words):
    a = list(words)
    j, m = 16, 0x0000FFFF
    while j:
        k = 0
        while k < 32:
            t = (a[k] ^ lax.shift_right_logical(a[k + j], j)) & m
            a[k] = a[k] ^ t
            a[k + j] = a[k + j] ^ lax.shift_left(t, j)
            k = (k + j + 1) & ~j
        j >>= 1
        m ^= (m << j) & 0xFFFFFFFF
    return a


def _mask_cap(sel):
    return jnp.where(sel, jnp.inf, NEG)


def _with_ones_rows(v_t):
    return jnp.concatenate([v_t, jnp.ones((ONES_ROWS, v_t.shape[1]), v_t.dtype)], axis=0)


def _online_softmax_step_t(s_t, cap, v_ext, m_ref, acc_ref, n_heads):
    q = cap.shape[1]
    ps, alphas = [], []
    for h in range(n_heads):
        sl = slice(h * q, (h + 1) * q)
        sm = jnp.minimum(s_t[:, sl], cap)
        m_old = m_ref[:, sl]
        m_new = jnp.maximum(m_old, _max_rows(sm))
        ps.append(jnp.exp2(sm - m_new).astype(BF16))
        alphas.append(jnp.exp2(m_old - m_new))
        m_ref[:, sl] = m_new
    acc_ref[...] = (jnp.concatenate(alphas, axis=1) * acc_ref[...]
                    + _dot(v_ext, jnp.concatenate(ps, axis=1)))


def _pipelined_attention_t(n_chunks, groups, sa_ref, sb_ref, m_ref, acc_ref, n_heads):
    last = 2 * n_chunks - 1
    width = sa_ref.shape[1] // len(groups)
    lanes = [slice(g * width, (g + 1) * width) for g in range(len(groups))]

    def fetch(s_ref, i):
        for (scores_fn, _, _), ln in zip(groups, lanes):
            s_ref[:, ln] = scores_fn(i)

    def consume(s_ref, i):
        for (_, cap_fn, v_fn), ln in zip(groups, lanes):
            _online_softmax_step_t(s_ref[:, ln], cap_fn(i), _with_ones_rows(v_fn(i)),
                                   m_ref.at[:, ln], acc_ref.at[:, ln], n_heads)

    fetch(sa_ref, 0)

    def body(c, carry):
        i0 = 2 * c
        fetch(sb_ref, i0 + 1)
        consume(sa_ref, i0)
        fetch(sa_ref, jnp.minimum(i0 + 2, last))
        consume(sb_ref, i0 + 1)
        return carry

    lax.fori_loop(0, n_chunks, body, 0)


def _finish_online_softmax_t(m_ref, acc_ref):
    d = acc_ref.shape[0] - ONES_ROWS
    return acc_ref[:d, :] * (jnp.where(m_ref[...] > NEG, 1.0, 0.0) / acc_ref[d:d + 1, :])


def _masked_softmax_t(s, cap):
    sm = jnp.minimum(s, cap)
    mx = _max_rows(sm)
    e = jnp.exp2(sm - mx)
    return e * (jnp.where(mx > NEG, 1.0, 0.0) / _sum_rows(e))


def _dsa_kernel(qat_ref, qit_ref, misct_ref, ki_ref, ka_ref, vat_ref, o_ref,
                key_ref, plane_ref, m_ref, acc_ref, *, topk, idx_bits):
    qb = Q_BLOCK
    kc = KEY_CHUNK
    q0 = pl.program_id(1) * qb
    n_chunks = lax.div(q0 + qb + kc - 1, kc)
    tq = q0 + lax.broadcasted_iota(I32, (1, qb), 1)
    row_k = lax.broadcasted_iota(I32, (kc, qb), 0)
    wi = misct_ref[0, _MISC_WI:_MISC_WI + A_IDX_HEADS, :]
    qi_t = qit_ref[0]
    qi_pairs = [_heads_on_lanes(qi_t[p * 2 * A_IDX_DIM:(p + 1) * 2 * A_IDX_DIM], 2, A_IDX_DIM)
                for p in range(A_IDX_HEADS // 2)]

    @pl.when((pl.program_id(0) == 0) & (pl.program_id(1) == 0))
    def _():
        plane_ref[...] = jnp.zeros_like(plane_ref)

    def score_chunk(c, carry):
        k0 = pl.multiple_of(c * kc, kc)
        ki = ki_ref[0, pl.ds(k0, kc), :]
        acc = jnp.zeros((kc, qb), F32)
        for p in range(A_IDX_HEADS // 2):
            lg = _dot(ki, qi_pairs[p])
            for j in range(2):
                h = 2 * p + j
                acc = acc + wi[h:h + 1, :] * jnp.maximum(lg[:, j * qb:(j + 1) * qb], 0.0)
        sc = jnp.where(k0 + row_k <= tq, acc, NEG)
        sc = jnp.where(sc == 0.0, 0.0, sc)
        bits = pltpu.bitcast(sc, I32)
        key = jnp.where(bits < 0, bits ^ 0x7FFFFFFF, bits)
        key_ref[pl.ds(k0, kc), :] = key
        ukey = key ^ INT_MIN
        group = PLANE_BITS * SUBLANES
        for g in range(kc // group):
            rows = [ukey[g * group + v * SUBLANES:g * group + (v + 1) * SUBLANES, :]
                    for v in range(PLANE_BITS)]
            base = pl.multiple_of(c * (kc // PLANE_BITS) + g * SUBLANES, SUBLANES)
            for b, plane in enumerate(_bit_transpose32(rows)):
                plane_ref[b, pl.ds(base, SUBLANES), :] = plane
        return carry

    lax.fori_loop(0, n_chunks, score_chunk, 0)

    wide = 2 * kc
    n_wide = lax.div(q0 + qb + wide - 1, wide)
    row_w = lax.broadcasted_iota(I32, (wide, qb), 0)

    @pl.when(n_wide * 2 > n_chunks)
    def _():
        key_ref[pl.ds(pl.multiple_of(n_chunks * kc, kc), kc), :] = jnp.full((kc, qb), INT_MIN, I32)

    def count(pred):
        def body(c, acc):
            k0 = pl.multiple_of(c * wide, wide)
            hit = jnp.where(pred(key_ref[pl.ds(k0, wide), :], k0 + row_w), 1.0, 0.0)
            return acc + _fold_rows(hit, jnp.add)
        acc = lax.fori_loop(0, n_wide, body, jnp.zeros((SUBLANES, qb), F32))
        return jnp.sum(acc, axis=0, keepdims=True)

    n_words = n_chunks * (kc // PLANE_BITS)
    word_row = lax.broadcasted_iota(I32, (plane_ref.shape[1], qb), 0)

    def search(it, carry):
        gt, eq, thr_u, cnt_thr = carry
        hit = eq & plane_ref[it]
        ge = gt | hit
        ones = _fold_rows(lax.population_count(ge), jnp.add).astype(F32)
        cnt = jnp.sum(ones, axis=0, keepdims=True)
        ok = cnt >= topk
        bit = lax.shift_left(jnp.int32(1), PLANE_BITS - 1 - it)
        return (jnp.where(ok, gt, ge), jnp.where(ok, hit, eq ^ hit),
                jnp.where(ok, thr_u | bit, thr_u), jnp.where(ok, cnt, cnt_thr))

    eq0 = jnp.where(word_row < n_words, jnp.int32(-1), jnp.int32(0))
    cnt0 = jnp.full((1, qb), 1.0, F32) * (n_chunks * kc).astype(F32)
    _, _, thr_u, cnt_thr = lax.fori_loop(
        0, PLANE_BITS, search, (jnp.zeros_like(eq0), eq0, jnp.zeros((1, qb), I32), cnt0))
    thr = thr_u ^ INT_MIN

    has_tie = jnp.max(jnp.where((cnt_thr > topk) & (thr > NEG_KEY), 1.0, 0.0)) > 0.0

    def tie_search():
        need = topk - count(lambda key, pos: key > thr)

        def step(it, last):
            cand = last + lax.shift_left(jnp.int32(1), idx_bits - 1 - it)
            below = count(lambda key, pos: (key == thr) & (pos < cand))
            return jnp.where(below < need, cand, last)

        return lax.fori_loop(0, idx_bits, step, jnp.zeros((1, qb), I32))

    last_eq = lax.cond(has_tie, tie_search, lambda: jnp.full((1, qb), INT_MAX, I32))

    m_ref[...] = jnp.full_like(m_ref, NEG)
    acc_ref[...] = jnp.zeros_like(acc_ref)
    q_all = _heads_on_lanes(qat_ref[0], A_HEADS, HEAD_DIM)

    def attend(c, carry):
        k0 = pl.multiple_of(c * kc, kc)
        key = key_ref[pl.ds(k0, kc), :]
        pos = k0 + row_k
        sel = ((key > thr) | ((key == thr) & (pos <= last_eq))) & (pos <= tq)
        s_t = _dot(ka_ref[0, pl.ds(k0, kc), :], q_all)
        _online_softmax_step_t(s_t, _mask_cap(sel), _with_ones_rows(vat_ref[0, :, pl.ds(k0, kc)]),
                               m_ref, acc_ref, A_HEADS)
        return carry

    lax.fori_loop(0, n_chunks, attend, 0)
    o_t = _finish_online_softmax_t(m_ref, acc_ref)
    o_t = jnp.concatenate([o_t[:, h * qb:(h + 1) * qb] for h in range(A_HEADS)], axis=0)
    o_ref[0] = o_t.T.astype(o_ref.dtype)


def _dsa(qa_t, qi_t, misc_t, ki, ka, va_t):
    bsz, _, s = qa_t.shape
    qb = Q_BLOCK
    assert s % (2 * KEY_CHUNK) == 0 and KEY_CHUNK >= A_TOPK_MAX
    topk = min(A_TOPK_MAX, s // 4)
    idx_bits = int(np.ceil(np.log2(s)))
    feat = lambda rows: pl.BlockSpec((1, rows, qb), lambda b, n: (b, 0, n))
    return pl.pallas_call(
        functools.partial(_dsa_kernel, topk=topk, idx_bits=idx_bits),
        grid=(bsz, s // qb),
        in_specs=[feat(A_WIDTH), feat(A_IDX_HEADS * A_IDX_DIM), feat(_MISC_ROWS),
                  pl.BlockSpec((1, s, A_IDX_DIM), lambda b, n: (b, 0, 0)),
                  pl.BlockSpec((1, s, HEAD_DIM), lambda b, n: (b, 0, 0)),
                  pl.BlockSpec((1, HEAD_DIM, s), lambda b, n: (b, 0, 0))],
        out_specs=pl.BlockSpec((1, qb, A_WIDTH), lambda b, n: (b, n, 0)),
        out_shape=jax.ShapeDtypeStruct((bsz, s, A_WIDTH), BF16),
        scratch_shapes=[pltpu.VMEM((s, qb), I32),
                        pltpu.VMEM((PLANE_BITS, s // PLANE_BITS, qb), I32),
                        pltpu.VMEM((1, A_HEADS * qb), F32),
                        pltpu.VMEM((HEAD_DIM + ONES_ROWS, A_HEADS * qb), F32)],
        compiler_params=_params(2),
        name="dsa",
    )(qa_t, qi_t, misc_t, ki, ka, va_t)


def _compress_kernel(xk_ref, xv_ref, pek_ref, pev_ref, w1k_ref, w1v_ref, w2k_ref, w2vt_ref,
                     ok_ref, ovt_ref):
    def hidden(x_ref, pe_ref, w1_ref):
        x = x_ref[0, 0]
        n = x.shape[0]
        nxt = pltpu.roll(x, n - 1, 0)
        a = (x + pe_ref[0:1, :]).astype(BF16)
        b = (nxt + pe_ref[1:2, :]).astype(BF16)
        hid = _dot(a, w1_ref[0]) + _dot(b, w1_ref[1])
        return (hid * jax.nn.sigmoid(hid)).astype(BF16)

    ok_ref[0, 0] = _dot(hidden(xk_ref, pek_ref, w1k_ref), w2k_ref[...]).astype(ok_ref.dtype)
    ovt_ref[0, 0] = _dot_nt(w2vt_ref[...], hidden(xv_ref, pev_ref, w1v_ref)).astype(ovt_ref.dtype)


def _compress(cmp_in, pe_k, w1_k, w2_k, pe_v, w1_v, w2_v):
    bsz, s, _ = cmp_in.shape
    nch = s // B_CMP_STRIDE
    cw = B_CMP_STRIDE * HEAD_DIM

    def chunks(t):
        t = t.reshape(bsz, nch, B_CMP_STRIDE, B_GROUPS, HEAD_DIM)
        return t.transpose(0, 3, 1, 2, 4).reshape(bsz, B_GROUPS, nch, cw)

    xk = chunks(cmp_in[:, :, :LANES])
    xv = chunks(cmp_in[:, :, LANES:])
    xspec = pl.BlockSpec((1, 1, nch, cw), lambda b, g: (b, g, 0, 0))
    c2 = lambda b, g: (0, 0)
    c3 = lambda b, g: (0, 0, 0)
    return pl.pallas_call(
        _compress_kernel,
        grid=(bsz, B_GROUPS),
        in_specs=[xspec, xspec,
                  pl.BlockSpec((2, cw), c2), pl.BlockSpec((2, cw), c2),
                  pl.BlockSpec((2, cw, B_CMP_HIDDEN), c3), pl.BlockSpec((2, cw, B_CMP_HIDDEN), c3),
                  pl.BlockSpec((B_CMP_HIDDEN, HEAD_DIM), c2), pl.BlockSpec((HEAD_DIM, B_CMP_HIDDEN), c2)],
        out_specs=[pl.BlockSpec((1, 1, nch, HEAD_DIM), lambda b, g: (b, g, 0, 0)),
                   pl.BlockSpec((1, 1, HEAD_DIM, nch), lambda b, g: (b, g, 0, 0))],
        out_shape=[jax.ShapeDtypeStruct((bsz, B_GROUPS, nch, HEAD_DIM), BF16),
                   jax.ShapeDtypeStruct((bsz, B_GROUPS, HEAD_DIM, nch), BF16)],
        compiler_params=_params(2),
        name="compress",
    )(xk, xv, pe_k.reshape(2, cw), pe_v.reshape(2, cw),
      w1_k.astype(BF16).reshape(2, cw, B_CMP_HIDDEN), w1_v.astype(BF16).reshape(2, cw, B_CMP_HIDDEN),
      w2_k.astype(BF16), w2_v.T.astype(BF16))


def _nsa_kernel(qbt_ref, misct_ref, kcmp_ref, vcmpt_ref, ks_ref, vst_ref, kw_ref, vwt_ref, ovlt_ref,
                o_ref, chosen_ref, sa_ref, sb_ref, m_ref, acc_ref, out_ref, *, n_slc, top_n):
    qb = Q_BLOCK
    kc = KEY_CHUNK
    rep = B_REP
    slc_shift = int(np.log2(B_SLC_LEN))
    q0 = pl.program_id(1) * qb
    n_chunks = lax.div(q0 + qb + kc - 1, kc)
    tq = q0 + lax.broadcasted_iota(I32, (1, qb), 1)
    gates = misct_ref[0]
    n_cmp = kcmp_ref.shape[2]
    n_rows = chosen_ref.shape[1]
    cmp_end = lax.broadcasted_iota(I32, (n_cmp, qb), 0) * B_CMP_STRIDE + (B_CMP_LEN - 1)
    cap_c = _mask_cap(cmp_end <= tq)
    jj = lax.broadcasted_iota(I32, (n_rows, qb), 0)
    jj_f = jj.astype(F32)
    blk_q = lax.shift_right_logical(tq, slc_shift)
    forced = (jj == 0) | (jj == blk_q) | (jj == blk_q - 1)
    half = kc // 2
    row_h = lax.broadcasted_iota(I32, (half, qb), 0)
    w0 = pl.multiple_of(jnp.maximum(q0 - B_WINDOW, 0), qb)
    wlen = B_WINDOW + qb
    pos_w = w0 + lax.broadcasted_iota(I32, (wlen, qb), 0)
    cap_w = _mask_cap((pos_w <= tq) & (pos_w > tq - B_WINDOW))

    q4s = [_heads_on_lanes(qbt_ref[0, g * rep * HEAD_DIM:(g + 1) * rep * HEAD_DIM, :], rep, HEAD_DIM)
           for g in range(B_GROUPS)]
    vrows = [slice(g * HEAD_DIM, (g + 1) * HEAD_DIM) for g in range(B_GROUPS)]

    def add_branch(g, branch, o_t, first=False):
        for r in range(rep):
            h = g * rep + r
            rows = slice(h * HEAD_DIM, (h + 1) * HEAD_DIM)
            term = gates[branch * B_HEADS + h:branch * B_HEADS + h + 1, :] * o_t[:, r * qb:(r + 1) * qb]
            out_ref[rows, :] = term if first else out_ref[rows, :] + term

    for g in range(B_GROUPS):
        q4 = q4s[g]
        s_c = _dot(kcmp_ref[0, g], q4)
        p_c = [_masked_softmax_t(s_c[:, r * qb:(r + 1) * qb], cap_c) for r in range(rep)]
        add_branch(g, 0, _dot(vcmpt_ref[0, g], jnp.concatenate([p.astype(BF16) for p in p_c], axis=1)),
                   first=True)

        p_sum = p_c[0]
        for r in range(1, rep):
            p_sum = p_sum + p_c[r]
        hi = p_sum.astype(BF16)
        r1 = p_sum - hi.astype(F32)
        mid = r1.astype(BF16)
        lo = (r1 - mid.astype(F32)).astype(BF16)
        ovl = ovlt_ref[...]
        imp = _dot(ovl, hi) + _dot(ovl, mid) + _dot(ovl, lo)
        val = jnp.where(jj <= blk_q, imp + FORCE_BONUS * jnp.where(forced, 1.0, 0.0), NEG)
        val = jnp.where(jj < n_slc, val, -jnp.inf)

        def pick(i, carry):
            cur, chosen = carry
            mx = jnp.max(cur, axis=0, keepdims=True)
            first = jnp.min(jnp.where(cur == mx, jj_f, float(n_rows)), axis=0, keepdims=True)
            hit = jj_f == first
            return jnp.where(hit, -jnp.inf, cur), jnp.where(hit, 1.0, chosen)

        _, chosen = lax.fori_loop(0, top_n, pick, (val, jnp.zeros((n_rows, qb), F32)))
        chosen_ref[g] = chosen

        s_w = _dot(kw_ref[0, g, pl.ds(w0, wlen), :], q4)
        p_w = [_masked_softmax_t(s_w[:, r * qb:(r + 1) * qb], cap_w).astype(BF16) for r in range(rep)]
        add_branch(g, 2, _dot(vwt_ref[0, vrows[g], pl.ds(w0, wlen)], jnp.concatenate(p_w, axis=1)))

    m_ref[...] = jnp.full_like(m_ref, NEG)
    acc_ref[...] = jnp.zeros_like(acc_ref)

    def keys_at(i):
        return pl.ds(pl.multiple_of(i * half, half), half)

    def group_fns(g):
        def cap_fn(i):
            j0 = i * (half // B_SLC_LEN)
            blocks = [jnp.broadcast_to(chosen_ref[g, pl.ds(j0 + b, 1), :], (B_SLC_LEN, qb))
                      for b in range(half // B_SLC_LEN)]
            return _mask_cap((jnp.concatenate(blocks, axis=0) > 0.5) & (i * half + row_h <= tq))
        return (lambda i: _dot(ks_ref[0, g, keys_at(i), :], q4s[g]), cap_fn,
                lambda i: vst_ref[0, vrows[g], keys_at(i)])

    _pipelined_attention_t(n_chunks, [group_fns(g) for g in range(B_GROUPS)],
                           sa_ref, sb_ref, m_ref, acc_ref, rep)
    o_s = _finish_online_softmax_t(m_ref, acc_ref)
    for g in range(B_GROUPS):
        add_branch(g, 1, o_s[:, g * rep * qb:(g + 1) * rep * qb])

    o_ref[0] = out_ref[...].T.astype(o_ref.dtype)


def _nsa(qb_t, misc_t, kcmp, vcmp_t, ks, vs_t, kw, vw_t):
    bsz, _, s = qb_t.shape
    qblk = Q_BLOCK
    n_cmp = kcmp.shape[2]
    n_slc = s // B_SLC_LEN
    top_n = min(B_SLC_TOP, n_slc)
    n_rows = -(-n_slc // SUBLANES) * SUBLANES
    assert s % KEY_CHUNK == 0 and s >= B_WINDOW + qblk and KEY_CHUNK % B_SLC_LEN == 0
    slc_start = np.arange(n_rows)[:, None] * B_SLC_LEN
    cmp_start = np.arange(n_cmp)[None, :] * B_CMP_STRIDE
    overlap_t = ((cmp_start < slc_start + B_SLC_LEN) & (slc_start < cmp_start + B_CMP_LEN)
                 & (np.arange(n_rows)[:, None] < n_slc))
    overlap_t = jnp.asarray(overlap_t, BF16)
    feat = lambda rows: pl.BlockSpec((1, rows, qblk), lambda b, n: (b, 0, n))
    k_spec = pl.BlockSpec((1, B_GROUPS, s, HEAD_DIM), lambda b, n: (b, 0, 0, 0))
    vt_spec = pl.BlockSpec((1, B_KV_WIDTH, s), lambda b, n: (b, 0, 0))
    return pl.pallas_call(
        functools.partial(_nsa_kernel, n_slc=n_slc, top_n=top_n),
        grid=(bsz, s // qblk),
        in_specs=[feat(B_WIDTH), feat(_MISC_ROWS),
                  pl.BlockSpec((1, B_GROUPS, n_cmp, HEAD_DIM), lambda b, n: (b, 0, 0, 0)),
                  pl.BlockSpec((1, B_GROUPS, HEAD_DIM, n_cmp), lambda b, n: (b, 0, 0, 0)),
                  k_spec, vt_spec, k_spec, vt_spec,
                  pl.BlockSpec((n_rows, n_cmp), lambda b, n: (0, 0))],
        out_specs=pl.BlockSpec((1, qblk, B_WIDTH), lambda b, n: (b, n, 0)),
        out_shape=jax.ShapeDtypeStruct((bsz, s, B_WIDTH), BF16),
        scratch_shapes=[pltpu.VMEM((B_GROUPS, n_rows, qblk), F32),
                        pltpu.VMEM((KEY_CHUNK // 2, B_HEADS * qblk), F32),
                        pltpu.VMEM((KEY_CHUNK // 2, B_HEADS * qblk), F32),
                        pltpu.VMEM((1, B_HEADS * qblk), F32),
                        pltpu.VMEM((HEAD_DIM + ONES_ROWS, B_HEADS * qblk), F32),
                        pltpu.VMEM((B_WIDTH, qblk), F32)],
        compiler_params=_params(2),
        name="nsa",
    )(qb_t, misc_t, kcmp, vcmp_t, ks, vs_t, kw, vw_t, overlap_t)


def _merge_kernel(x_ref, mod_ref, ya_ref, yb_ref, wgm_ref, wa_ref, wb_ref, wo_ref, g_ref, b_ref,
                  o_ref, *, alpha):
    x = x_ref[0]
    d = x.shape[-1]
    sh = mod_ref[0, 3:4, :]
    sc = mod_ref[0, 4:5, :]
    gt = mod_ref[0, 5:6, :]
    u = (x * (1.0 + sc) + sh).astype(BF16)
    gm = jax.nn.sigmoid(_dot(u, wgm_ref[...]))
    merged = gm[:, :d] * _dot(ya_ref[0], wa_ref[...]) + gm[:, d:] * _dot(yb_ref[0], wb_ref[...])
    mix = _dot(merged.astype(BF16), wo_ref[...])
    o_ref[0] = _layer_norm(alpha * x + gt * mix, g_ref[...], b_ref[...])


def _merge(x, mod, ya, yb, w_gm, w_br_a, w_br_b, w_mix_out, ln_g, ln_b, alpha):
    bsz, s, d = x.shape
    tm = TOKEN_TILE
    c2 = lambda b, i: (0, 0)
    tok = lambda width: pl.BlockSpec((1, tm, width), lambda b, i: (b, i, 0))
    return pl.pallas_call(
        functools.partial(_merge_kernel, alpha=alpha),
        grid=(bsz, s // tm),
        in_specs=[tok(d),
                  pl.BlockSpec((1, N_MOD, d), lambda b, i: (b, 0, 0)),
                  tok(A_WIDTH), tok(B_WIDTH),
                  pl.BlockSpec((d, 2 * d), c2),
                  pl.BlockSpec((A_WIDTH, d), c2),
                  pl.BlockSpec((B_WIDTH, d), c2),
                  pl.BlockSpec((d, d), c2),
                  pl.BlockSpec((1, d), c2), pl.BlockSpec((1, d), c2)],
        out_specs=tok(d),
        out_shape=jax.ShapeDtypeStruct((bsz, s, d), F32),
        compiler_params=_params(2),
        name="merge",
    )(x, mod, ya, yb, w_gm.astype(BF16), w_br_a.astype(BF16), w_br_b.astype(BF16),
      w_mix_out.astype(BF16), ln_g.reshape(1, d), ln_b.reshape(1, d))


def kernel(x, c, w_ada, b_ada, ffn1_w_in, ffn1_w_out, ln1_g, ln1_b, w_mix_in, kv_norm_g, w_kv_up,
           cmp_pe_k, cmp_w1_k, cmp_w2_k, cmp_pe_v, cmp_w1_v, cmp_w2_v, w_br_a, w_br_b, w_mix_out,
           ln2_g, ln2_b, ffn2_w_in, ffn2_w_out, ln3_g, ln3_b):
    bsz, s, d = x.shape
    depth = w_ada.shape[0]
    alpha = (2.0 * depth) ** 0.25
    assert s % TOKEN_TILE == 0 and s % KEY_CHUNK == 0
    for l in range(depth):
        mod = _ada(c, w_ada[l], b_ada[l]).reshape(bsz, N_MOD, d)
        x = _ffn(x, mod, ffn1_w_in[l], ffn1_w_out[l], ln1_g[l], ln1_b[l], 0, alpha)
        qa_t, qb_t, qi_t, va_t, vs_t, vw_t, misc_t, ka, ki, ks, kw, cmp_in = _mixin(
            x, mod, w_mix_in[l], kv_norm_g[l], w_kv_up[l])
        ya = _dsa(qa_t, qi_t, misc_t, ki, ka, va_t)
        kcmp, vcmp_t = _compress(cmp_in, cmp_pe_k[l], cmp_w1_k[l], cmp_w2_k[l],
                                 cmp_pe_v[l], cmp_w1_v[l], cmp_w2_v[l])
        yb = _nsa(qb_t, misc_t, kcmp, vcmp_t, ks, vs_t, kw, vw_t)
        w_gm = w_mix_in[l][:, -2 * d:]
        x = _merge(x, mod, ya, yb, w_gm, w_br_a[l], w_br_b[l], w_mix_out[l], ln2_g[l], ln2_b[l], alpha)
        x = _ffn(x, mod, ffn2_w_in[l], ffn2_w_out[l], ln3_g[l], ln3_b[l], 6, alpha)
    return x
```

```python
import functools

import numpy as np
import jax
import jax.numpy as jnp
from jax import lax
from jax.experimental import pallas as pl
from jax.experimental.pallas import tpu as pltpu

F32 = jnp.float32
BF16 = jnp.bfloat16
I32 = jnp.int32

HEAD_DIM = 64
ROPE_THETA = 10000.0
A_HEADS = 8
A_WIDTH = A_HEADS * HEAD_DIM
A_KV_RANK = 128
A_IDX_HEADS = 4
A_IDX_DIM = 64
A_TOPK_MAX = 256
B_HEADS = 8
B_GROUPS = 2
B_REP = B_HEADS // B_GROUPS
B_WIDTH = B_HEADS * HEAD_DIM
B_KV_WIDTH = B_GROUPS * HEAD_DIM
B_CMP_STRIDE = 16
B_CMP_LEN = 2 * B_CMP_STRIDE
B_CMP_HIDDEN = 256
B_SLC_LEN = 64
B_SLC_TOP = 16
B_WINDOW = 512
N_MOD = 9
LN_EPS = 1e-5
NEG = -1e30
FORCE_BONUS = 1e4

LANES = 128
SUBLANES = 8
Q_BLOCK = LANES
KEY_CHUNK = 512
TOKEN_TILE = 512
FF_CHUNK = 256
PLANE_BITS = 32
ONES_ROWS = 16
VMEM_LIMIT = 56 * 1024 * 1024

NEG_KEY = int(np.array(NEG, np.float32).view(np.int32)) ^ 0x7FFFFFFF
INT_MIN = -(2 ** 31)
INT_MAX = 2 ** 31 - 1


def _dot(a, b):
    return jnp.dot(a, b, preferred_element_type=F32)


def _dot_nt(a, b):
    return lax.dot_general(a, b, (((1,), (1,)), ((), ())), preferred_element_type=F32)


def _layer_norm(y, g, b):
    mu = jnp.mean(y, axis=-1, keepdims=True)
    d = y - mu
    var = jnp.mean(d * d, axis=-1, keepdims=True)
    return d * lax.rsqrt(var + LN_EPS) * g + b


def _params(n_axes):
    return pltpu.CompilerParams(dimension_semantics=("arbitrary",) * n_axes,
                                vmem_limit_bytes=VMEM_LIMIT)


def _ada_kernel(c_ref, w_ref, b_ref, o_ref):
    c = c_ref[...]
    ca = c * jax.nn.sigmoid(c)
    o_ref[...] = jnp.dot(ca, w_ref[...], preferred_element_type=F32,
                         precision=lax.Precision.HIGHEST) + b_ref[...]


def _ada(c, w_ada, b_ada):
    bsz, d = c.shape
    n = w_ada.shape[1]
    tn = n // 8
    return pl.pallas_call(
        _ada_kernel,
        grid=(n // tn,),
        in_specs=[pl.BlockSpec((bsz, d), lambda j: (0, 0)),
                  pl.BlockSpec((d, tn), lambda j: (0, j)),
                  pl.BlockSpec((1, tn), lambda j: (0, j))],
        out_specs=pl.BlockSpec((bsz, tn), lambda j: (0, j)),
        out_shape=jax.ShapeDtypeStruct((bsz, n), F32),
        compiler_params=_params(1),
        name="ada",
    )(c, w_ada, b_ada.reshape(1, n))


def _ffn_kernel(x_ref, mod_ref, wi_ref, wo_ref, g_ref, b_ref, o_ref, u_ref, acc_ref,
                *, mod_base, n_chunks, alpha):
    x = x_ref[0]
    sh = mod_ref[0, mod_base:mod_base + 1, :]
    sc = mod_ref[0, mod_base + 1:mod_base + 2, :]
    gt = mod_ref[0, mod_base + 2:mod_base + 3, :]
    u_ref[...] = (x * (1.0 + sc) + sh).astype(BF16)
    d_ff = wo_ref.shape[0]
    for j in range(n_chunks):
        cols = slice(j * FF_CHUNK, (j + 1) * FF_CHUNK)
        u = u_ref[...]
        a = _dot(u, wi_ref[:, cols])
        b = _dot(u, wi_ref[:, d_ff + j * FF_CHUNK:d_ff + (j + 1) * FF_CHUNK])
        h = (a * jax.nn.sigmoid(a) * b).astype(BF16)
        d = _dot(h, wo_ref[cols, :])
        if j == 0:
            acc_ref[...] = d
        else:
            acc_ref[...] += d
    y = alpha * x + (0.5 * gt) * acc_ref[...]
    o_ref[0] = _layer_norm(y, g_ref[...], b_ref[...])


def _ffn(x, mod, w_in, w_out, ln_g, ln_b, mod_base, alpha):
    bsz, s, d = x.shape
    d_ff = w_out.shape[0]
    n_chunks = d_ff // FF_CHUNK
    tm = TOKEN_TILE
    assert d_ff % FF_CHUNK == 0 and d_ff % LANES == 0
    const2 = lambda b, i: (0, 0)
    return pl.pallas_call(
        functools.partial(_ffn_kernel, mod_base=mod_base, n_chunks=n_chunks, alpha=alpha),
        grid=(bsz, s // tm),
        in_specs=[pl.BlockSpec((1, tm, d), lambda b, i: (b, i, 0)),
                  pl.BlockSpec((1, N_MOD, d), lambda b, i: (b, 0, 0)),
                  pl.BlockSpec((d, 2 * d_ff), const2),
                  pl.BlockSpec((d_ff, d), const2),
                  pl.BlockSpec((1, d), lambda b, i: (0, 0)),
                  pl.BlockSpec((1, d), lambda b, i: (0, 0))],
        out_specs=pl.BlockSpec((1, tm, d), lambda b, i: (b, i, 0)),
        out_shape=jax.ShapeDtypeStruct((bsz, s, d), F32),
        scratch_shapes=[pltpu.VMEM((tm, d), BF16), pltpu.VMEM((tm, d), F32)],
        compiler_params=_params(2),
        name="ffn",
    )(x, mod, w_in.astype(BF16), w_out.astype(BF16), ln_g.reshape(1, d), ln_b.reshape(1, d))


_S_KC = 0
_S_VC = _S_KC + B_KV_WIDTH
_S_KS = _S_VC + B_KV_WIDTH
_S_KW = _S_KS + B_KV_WIDTH
_S_LAT = _S_KW + B_KV_WIDTH
_S_KI = _S_LAT + A_KV_RANK
_P_STD = _S_KI + LANES
_T_QA = 0
_T_QB = _T_QA + A_WIDTH
_T_QI = _T_QB + B_WIDTH
_T_VS = _T_QI + A_IDX_HEADS * A_IDX_DIM
_T_VW = _T_VS + B_KV_WIDTH
_T_MISC = _T_VW + B_KV_WIDTH
_MISC_ROWS = 32
_MISC_WI = 3 * B_HEADS
_P_T = _T_MISC + _MISC_ROWS


def _mixin_kernel(x_ref, mod_ref, ws_ref, wt_ref, cos_ref, sin_ref, cost_ref, sint_ref,
                  kvg_ref, wk_ref, wvt_ref,
                  qat_ref, qbt_ref, qit_ref, vat_ref, vst_ref, vwt_ref, misct_ref,
                  ka_ref, ki_ref, ks_ref, kw_ref, cmp_ref, hs_ref, ht_ref):
    x = x_ref[0]
    sh = mod_ref[0, 3:4, :]
    sc = mod_ref[0, 4:5, :]
    u = (x * (1.0 + sc) + sh).astype(BF16)
    hs_ref[...] = _dot(u, ws_ref[...])
    ht_ref[...] = _dot_nt(wt_ref[...], u)

    tm = x.shape[0]
    half = HEAD_DIM // 2
    scale = HEAD_DIM ** -0.5 * float(np.log2(np.e))

    cos_t = cost_ref[...]
    sin_t = sint_ref[...]

    def rope_t(src0, dst_ref, dst0, mul):
        x1 = ht_ref[src0:src0 + half, :]
        x2 = ht_ref[src0 + half:src0 + HEAD_DIM, :]
        o1 = x1 * cos_t - x2 * sin_t
        o2 = x1 * sin_t + x2 * cos_t
        if mul != 1.0:
            o1 = o1 * mul
            o2 = o2 * mul
        dst_ref[0, dst0:dst0 + half, :] = o1.astype(dst_ref.dtype)
        dst_ref[0, dst0 + half:dst0 + HEAD_DIM, :] = o2.astype(dst_ref.dtype)

    for h in range(A_HEADS):
        rope_t(_T_QA + h * HEAD_DIM, qat_ref, h * HEAD_DIM, scale)
    for h in range(B_HEADS):
        rope_t(_T_QB + h * HEAD_DIM, qbt_ref, h * HEAD_DIM, scale)
    for h in range(A_IDX_HEADS):
        rope_t(_T_QI + h * A_IDX_DIM, qit_ref, h * A_IDX_DIM, 1.0)
    vst_ref[0] = ht_ref[_T_VS:_T_VS + B_KV_WIDTH, :].astype(BF16)
    vwt_ref[0] = ht_ref[_T_VW:_T_VW + B_KV_WIDTH, :].astype(BF16)
    misc = ht_ref[_T_MISC:_T_MISC + _MISC_ROWS, :]
    row = lax.broadcasted_iota(I32, misc.shape, 0)
    misct_ref[0] = jnp.where(row < _MISC_WI, jax.nn.sigmoid(misc), misc * (A_IDX_HEADS ** -0.5))

    cos = cos_ref[...]
    sin = sin_ref[...]
    lane = lax.broadcasted_iota(I32, (tm, LANES), 1)
    first_half = (lane & (HEAD_DIM - 1)) < half

    def rope(t):
        rot = jnp.where(first_half, pltpu.roll(t, LANES - half, 1), pltpu.roll(t, half, 1))
        return t * cos + rot * sin

    def chunk(c0):
        return hs_ref[:, c0:c0 + LANES]

    cmp_ref[0, :, :LANES] = rope(chunk(_S_KC))
    cmp_ref[0, :, LANES:] = chunk(_S_VC)
    ks = rope(chunk(_S_KS)).astype(BF16)
    kw = rope(chunk(_S_KW)).astype(BF16)
    for g in range(B_GROUPS):
        ks_ref[0, g] = ks[:, g * HEAD_DIM:(g + 1) * HEAD_DIM]
        kw_ref[0, g] = kw[:, g * HEAD_DIM:(g + 1) * HEAD_DIM]
    ki_ref[0] = rope(chunk(_S_KI))[:, :A_IDX_DIM].astype(BF16)

    lat = chunk(_S_LAT)
    latn = (lat * lax.rsqrt(jnp.mean(lat * lat, axis=-1, keepdims=True) + LN_EPS)
            * kvg_ref[...]).astype(BF16)
    ka_ref[0] = rope(_dot(latn, wk_ref[...]))[:, :HEAD_DIM].astype(BF16)
    vat_ref[0] = _dot_nt(wvt_ref[...], latn).astype(BF16)


def _rope_tables(s):
    half = HEAD_DIM // 2
    inv = ROPE_THETA ** (-jnp.arange(half, dtype=F32) / half)
    ang = jnp.arange(s).astype(F32)[:, None] * inv[None, :]
    cos = jnp.cos(ang)
    sin = jnp.sin(ang)
    reps = LANES // HEAD_DIM
    cos_l = jnp.tile(jnp.concatenate([cos, cos], axis=-1), (1, reps))
    sin_l = jnp.tile(jnp.concatenate([-sin, sin], axis=-1), (1, reps))
    return cos_l, sin_l, cos.T, sin.T


def _mixin(x, mod, w_mix_in, kv_norm_g, w_kv_up):
    bsz, s, d = x.shape
    tm = TOKEN_TILE
    split = np.cumsum([A_WIDTH, A_KV_RANK, A_IDX_HEADS * A_IDX_DIM, A_IDX_DIM, A_IDX_HEADS,
                       B_WIDTH, 6 * B_KV_WIDTH, 3 * B_HEADS]).tolist()
    w_qa, w_lat, w_qi, w_ki, w_wi, w_qb, w_kv6, w_gb, _ = jnp.split(w_mix_in, split, axis=-1)
    w_kc, w_vc, w_ks, w_vs, w_kw, w_vw = jnp.split(w_kv6, 6, axis=-1)
    zeros = lambda n: jnp.zeros((d, n), w_mix_in.dtype)
    w_std = jnp.concatenate([w_kc, w_vc, w_ks, w_kw, w_lat, w_ki, zeros(LANES - A_IDX_DIM)],
                            axis=-1).astype(BF16)
    w_t = jnp.concatenate([w_qa, w_qb, w_qi, w_vs, w_vw, w_gb, w_wi,
                           zeros(_MISC_ROWS - 3 * B_HEADS - A_IDX_HEADS)], axis=-1).T.astype(BF16)
    assert w_std.shape[1] == _P_STD and w_t.shape[0] == _P_T
    cos_l, sin_l, cos_t, sin_t = _rope_tables(s)
    w_k = jnp.concatenate([w_kv_up[:, :HEAD_DIM], jnp.zeros((A_KV_RANK, LANES - HEAD_DIM), F32)],
                          axis=-1).astype(BF16)
    w_vt = w_kv_up[:, HEAD_DIM:].T.astype(BF16)
    c2 = lambda b, i: (0, 0)
    feat = lambda rows: pl.BlockSpec((1, rows, tm), lambda b, i: (b, 0, i))
    tok = lambda width: pl.BlockSpec((1, tm, width), lambda b, i: (b, i, 0))
    grp = pl.BlockSpec((1, B_GROUPS, tm, HEAD_DIM), lambda b, i: (b, 0, i, 0))
    sds = jax.ShapeDtypeStruct
    out_shape = [sds((bsz, A_WIDTH, s), BF16),
                 sds((bsz, B_WIDTH, s), BF16),
                 sds((bsz, A_IDX_HEADS * A_IDX_DIM, s), BF16),
                 sds((bsz, HEAD_DIM, s), BF16),
                 sds((bsz, B_KV_WIDTH, s), BF16),
                 sds((bsz, B_KV_WIDTH, s), BF16),
                 sds((bsz, _MISC_ROWS, s), F32),
                 sds((bsz, s, HEAD_DIM), BF16),
                 sds((bsz, s, A_IDX_DIM), BF16),
                 sds((bsz, B_GROUPS, s, HEAD_DIM), BF16),
                 sds((bsz, B_GROUPS, s, HEAD_DIM), BF16),
                 sds((bsz, s, 2 * LANES), F32)]
    out_specs = [feat(A_WIDTH), feat(B_WIDTH), feat(A_IDX_HEADS * A_IDX_DIM), feat(HEAD_DIM),
                 feat(B_KV_WIDTH), feat(B_KV_WIDTH), feat(_MISC_ROWS),
                 tok(HEAD_DIM), tok(A_IDX_DIM), grp, grp, tok(2 * LANES)]
    return pl.pallas_call(
        _mixin_kernel,
        grid=(bsz, s // tm),
        in_specs=[pl.BlockSpec((1, tm, d), lambda b, i: (b, i, 0)),
                  pl.BlockSpec((1, N_MOD, d), lambda b, i: (b, 0, 0)),
                  pl.BlockSpec((d, _P_STD), c2),
                  pl.BlockSpec((_P_T, d), c2),
                  pl.BlockSpec((tm, LANES), lambda b, i: (i, 0)),
                  pl.BlockSpec((tm, LANES), lambda b, i: (i, 0)),
                  pl.BlockSpec((HEAD_DIM // 2, tm), lambda b, i: (0, i)),
                  pl.BlockSpec((HEAD_DIM // 2, tm), lambda b, i: (0, i)),
                  pl.BlockSpec((1, A_KV_RANK), c2),
                  pl.BlockSpec((A_KV_RANK, LANES), c2),
                  pl.BlockSpec((HEAD_DIM, A_KV_RANK), c2)],
        out_specs=out_specs,
        out_shape=out_shape,
        scratch_shapes=[pltpu.VMEM((tm, _P_STD), F32), pltpu.VMEM((_P_T, tm), F32)],
        compiler_params=_params(2),
        name="mixin",
    )(x, mod, w_std, w_t, cos_l, sin_l, cos_t, sin_t,
      kv_norm_g.reshape(1, A_KV_RANK), w_k, w_vt)


def _heads_on_lanes(qt, n_heads, dim):
    return jnp.concatenate([qt[h * dim:(h + 1) * dim, :] for h in range(n_heads)], axis=1)


def _fold_rows(x, op, tile=SUBLANES):
    while x.shape[0] % (2 * tile) == 0:
        h = x.shape[0] // 2
        x = op(x[:h], x[h:])
    return x


def _max_rows(x):
    return jnp.max(_fold_rows(x, jnp.maximum), axis=0, keepdims=True)


def _sum_rows(x):
    return jnp.sum(_fold_rows(x, jnp.add), axis=0, keepdims=True)


def _bit_transpose32(words):
    a = list(words)
    j, m = 16, 0x0000FFFF
    while j:
        k = 0
        while k < 32:
            t = (a[k] ^ lax.shift_right_logical(a[k + j], j)) & m
            a[k] = a[k] ^ t
            a[k + j] = a[k + j] ^ lax.shift_left(t, j)
            k = (k + j + 1) & ~j
        j >>= 1
        m ^= (m << j) & 0xFFFFFFFF
    return a


def _mask_cap(sel):
    return jnp.where(sel, jnp.inf, NEG)


def _with_ones_rows(v_t):
    return jnp.concatenate([v_t, jnp.ones((ONES_ROWS, v_t.shape[1]), v_t.dtype)], axis=0)


def _online_softmax_step_t(s_t, cap, v_ext, m_ref, acc_ref, n_heads):
    q = cap.shape[1]
    ps, alphas = [], []
    for h in range(n_heads):
        sl = slice(h * q, (h + 1) * q)
        sm = jnp.minimum(s_t[:, sl], cap)
        m_old = m_ref[:, sl]
        m_new = jnp.maximum(m_old, _max_rows(sm))
        ps.append(jnp.exp2(sm - m_new).astype(BF16))
        alphas.append(jnp.exp2(m_old - m_new))
        m_ref[:, sl] = m_new
    acc_ref[...] = (jnp.concatenate(alphas, axis=1) * acc_ref[...]
                    + _dot(v_ext, jnp.concatenate(ps, axis=1)))


def _pipelined_attention_t(n_chunks, groups, sa_ref, sb_ref, m_ref, acc_ref, n_heads):
    last = 2 * n_chunks - 1
    width = sa_ref.shape[1] // len(groups)
    lanes = [slice(g * width, (g + 1) * width) for g in range(len(groups))]

    def fetch(s_ref, i):
        for (scores_fn, _, _), ln in zip(groups, lanes):
            s_ref[:, ln] = scores_fn(i)

    def consume(s_ref, i):
        for (_, cap_fn, v_fn), ln in zip(groups, lanes):
            _online_softmax_step_t(s_ref[:, ln], cap_fn(i), _with_ones_rows(v_fn(i)),
                                   m_ref.at[:, ln], acc_ref.at[:, ln], n_heads)

    fetch(sa_ref, 0)

    def body(c, carry):
        i0 = 2 * c
        fetch(sb_ref, i0 + 1)
        consume(sa_ref, i0)
        fetch(sa_ref, jnp.minimum(i0 + 2, last))
        consume(sb_ref, i0 + 1)
        return carry

    lax.fori_loop(0, n_chunks, body, 0)


def _finish_online_softmax_t(m_ref, acc_ref):
    d = acc_ref.shape[0] - ONES_ROWS
    return acc_ref[:d, :] * (jnp.where(m_ref[...] > NEG, 1.0, 0.0) / acc_ref[d:d + 1, :])


def _masked_softmax_t(s, cap):
    sm = jnp.minimum(s, cap)
    mx = _max_rows(sm)
    e = jnp.exp2(sm - mx)
    return e * (jnp.where(mx > NEG, 1.0, 0.0) / _sum_rows(e))


def _dsa_kernel(qat_ref, qit_ref, misct_ref, ki_ref, ka_ref, vat_ref, o_ref,
                key_ref, plane_ref, m_ref, acc_ref, *, topk, idx_bits):
    qb = Q_BLOCK
    kc = KEY_CHUNK
    q0 = pl.program_id(1) * qb
    n_chunks = lax.div(q0 + qb + kc - 1, kc)
    tq = q0 + lax.broadcasted_iota(I32, (1, qb), 1)
    row_k = lax.broadcasted_iota(I32, (kc, qb), 0)
    wi = misct_ref[0, _MISC_WI:_MISC_WI + A_IDX_HEADS, :]
    qi_t = qit_ref[0]
    qi_pairs = [_heads_on_lanes(qi_t[p * 2 * A_IDX_DIM:(p + 1) * 2 * A_IDX_DIM], 2, A_IDX_DIM)
                for p in range(A_IDX_HEADS // 2)]

    @pl.when((pl.program_id(0) == 0) & (pl.program_id(1) == 0))
    def _():
        plane_ref[...] = jnp.zeros_like(plane_ref)

    def score_chunk(c, carry):
        k0 = pl.multiple_of(c * kc, kc)
        ki = ki_ref[0, pl.ds(k0, kc), :]
        acc = jnp.zeros((kc, qb), F32)
        for p in range(A_IDX_HEADS // 2):
            lg = _dot(ki, qi_pairs[p])
            for j in range(2):
                h = 2 * p + j
                acc = acc + wi[h:h + 1, :] * jnp.maximum(lg[:, j * qb:(j + 1) * qb], 0.0)
        sc = jnp.where(k0 + row_k <= tq, acc, NEG)
        sc = jnp.where(sc == 0.0, 0.0, sc)
        bits = pltpu.bitcast(sc, I32)
        key = jnp.where(bits < 0, bits ^ 0x7FFFFFFF, bits)
        key_ref[pl.ds(k0, kc), :] = key
        ukey = key ^ INT_MIN
        group = PLANE_BITS * SUBLANES
        for g in range(kc // group):
            rows = [ukey[g * group + v * SUBLANES:g * group + (v + 1) * SUBLANES, :]
                    for v in range(PLANE_BITS)]
            base = pl.multiple_of(c * (kc // PLANE_BITS) + g * SUBLANES, SUBLANES)
            for b, plane in enumerate(_bit_transpose32(rows)):
                plane_ref[b, pl.ds(base, SUBLANES), :] = plane
        return carry

    lax.fori_loop(0, n_chunks, score_chunk, 0)

    def ones_per_query(words):
        return jnp.sum(_fold_rows(lax.population_count(words), jnp.add).astype(F32), axis=0, keepdims=True)

    n_words = n_chunks * (kc // PLANE_BITS)
    word_row = lax.broadcasted_iota(I32, (plane_ref.shape[1], qb), 0)

    def search(it, carry):
        gt, eq, thr_u, cnt_thr = carry
        hit = eq & plane_ref[it]
        ge = gt | hit
        cnt = ones_per_query(ge)
        ok = cnt >= topk
        bit = lax.shift_left(jnp.int32(1), PLANE_BITS - 1 - it)
        return (jnp.where(ok, gt, ge), jnp.where(ok, hit, eq ^ hit),
                jnp.where(ok, thr_u | bit, thr_u), jnp.where(ok, cnt, cnt_thr))

    eq0 = jnp.where(word_row < n_words, jnp.int32(-1), jnp.int32(0))
    cnt0 = jnp.full((1, qb), 1.0, F32) * (n_chunks * kc).astype(F32)
    gt_thr, eq_thr, thr_u, cnt_thr = lax.fori_loop(
        0, PLANE_BITS, search, (jnp.zeros_like(eq0), eq0, jnp.zeros((1, qb), I32), cnt0))
    thr = thr_u ^ INT_MIN

    has_tie = jnp.max(jnp.where((cnt_thr > topk) & (thr > NEG_KEY), 1.0, 0.0)) > 0.0
    group_shift = int(np.log2(PLANE_BITS * SUBLANES))
    word_pos = (lax.shift_left(lax.shift_right_logical(word_row, 3), group_shift)
                + (word_row & (SUBLANES - 1)))

    def tie_search():
        need = topk - ones_per_query(gt_thr)

        def step(it, last):
            cand = last + lax.shift_left(jnp.int32(1), idx_bits - 1 - it)
            n_below = jnp.clip(lax.shift_right_arithmetic(cand - word_pos + (SUBLANES - 1), 3),
                               0, PLANE_BITS)
            low = lax.shift_right_logical(jnp.full_like(n_below, -1),
                                          jnp.minimum(n_below, PLANE_BITS - 1))
            top = jnp.where(n_below >= PLANE_BITS, jnp.int32(-1), ~low)
            below = ones_per_query(eq_thr & top)
            return jnp.where(below < need, cand, last)

        return lax.fori_loop(0, idx_bits, step, jnp.zeros((1, qb), I32))

    last_eq = lax.cond(has_tie, tie_search, lambda: jnp.full((1, qb), INT_MAX, I32))

    m_ref[...] = jnp.full_like(m_ref, NEG)
    acc_ref[...] = jnp.zeros_like(acc_ref)
    q_all = _heads_on_lanes(qat_ref[0], A_HEADS, HEAD_DIM)

    def attend(c, carry):
        k0 = pl.multiple_of(c * kc, kc)
        key = key_ref[pl.ds(k0, kc), :]
        pos = k0 + row_k
        sel = ((key > thr) | ((key == thr) & (pos <= last_eq))) & (pos <= tq)
        s_t = _dot(ka_ref[0, pl.ds(k0, kc), :], q_all)
        _online_softmax_step_t(s_t, _mask_cap(sel), _with_ones_rows(vat_ref[0, :, pl.ds(k0, kc)]),
                               m_ref, acc_ref, A_HEADS)
        return carry

    lax.fori_loop(0, n_chunks, attend, 0)
    o_t = _finish_online_softmax_t(m_ref, acc_ref)
    o_t = jnp.concatenate([o_t[:, h * qb:(h + 1) * qb] for h in range(A_HEADS)], axis=0)
    o_ref[0] = o_t.T.astype(o_ref.dtype)


def _dsa(qa_t, qi_t, misc_t, ki, ka, va_t):
    bsz, _, s = qa_t.shape
    qb = Q_BLOCK
    assert s % (2 * KEY_CHUNK) == 0 and KEY_CHUNK >= A_TOPK_MAX
    topk = min(A_TOPK_MAX, s // 4)
    idx_bits = int(np.ceil(np.log2(s)))
    feat = lambda rows: pl.BlockSpec((1, rows, qb), lambda b, n: (b, 0, n))
    return pl.pallas_call(
        functools.partial(_dsa_kernel, topk=topk, idx_bits=idx_bits),
        grid=(bsz, s // qb),
        in_specs=[feat(A_WIDTH), feat(A_IDX_HEADS * A_IDX_DIM), feat(_MISC_ROWS),
                  pl.BlockSpec((1, s, A_IDX_DIM), lambda b, n: (b, 0, 0)),
                  pl.BlockSpec((1, s, HEAD_DIM), lambda b, n: (b, 0, 0)),
                  pl.BlockSpec((1, HEAD_DIM, s), lambda b, n: (b, 0, 0))],
        out_specs=pl.BlockSpec((1, qb, A_WIDTH), lambda b, n: (b, n, 0)),
        out_shape=jax.ShapeDtypeStruct((bsz, s, A_WIDTH), BF16),
        scratch_shapes=[pltpu.VMEM((s, qb), I32),
                        pltpu.VMEM((PLANE_BITS, s // PLANE_BITS, qb), I32),
                        pltpu.VMEM((1, A_HEADS * qb), F32),
                        pltpu.VMEM((HEAD_DIM + ONES_ROWS, A_HEADS * qb), F32)],
        compiler_params=_params(2),
        name="dsa",
    )(qa_t, qi_t, misc_t, ki, ka, va_t)


def _compress_kernel(xk_ref, xv_ref, pek_ref, pev_ref, w1k_ref, w1v_ref, w2k_ref, w2vt_ref,
                     ok_ref, ovt_ref):
    def hidden(x_ref, pe_ref, w1_ref):
        x = x_ref[0, 0]
        n = x.shape[0]
        nxt = pltpu.roll(x, n - 1, 0)
        a = (x + pe_ref[0:1, :]).astype(BF16)
        b = (nxt + pe_ref[1:2, :]).astype(BF16)
        hid = _dot(a, w1_ref[0]) + _dot(b, w1_ref[1])
        return (hid * jax.nn.sigmoid(hid)).astype(BF16)

    ok_ref[0, 0] = _dot(hidden(xk_ref, pek_ref, w1k_ref), w2k_ref[...]).astype(ok_ref.dtype)
    ovt_ref[0, 0] = _dot_nt(w2vt_ref[...], hidden(xv_ref, pev_ref, w1v_ref)).astype(ovt_ref.dtype)


def _compress(cmp_in, pe_k, w1_k, w2_k, pe_v, w1_v, w2_v):
    bsz, s, _ = cmp_in.shape
    nch = s // B_CMP_STRIDE
    cw = B_CMP_STRIDE * HEAD_DIM

    def chunks(t):
        t = t.reshape(bsz, nch, B_CMP_STRIDE, B_GROUPS, HEAD_DIM)
        return t.transpose(0, 3, 1, 2, 4).reshape(bsz, B_GROUPS, nch, cw)

    xk = chunks(cmp_in[:, :, :LANES])
    xv = chunks(cmp_in[:, :, LANES:])
    xspec = pl.BlockSpec((1, 1, nch, cw), lambda b, g: (b, g, 0, 0))
    c2 = lambda b, g: (0, 0)
    c3 = lambda b, g: (0, 0, 0)
    return pl.pallas_call(
        _compress_kernel,
        grid=(bsz, B_GROUPS),
        in_specs=[xspec, xspec,
                  pl.BlockSpec((2, cw), c2), pl.BlockSpec((2, cw), c2),
                  pl.BlockSpec((2, cw, B_CMP_HIDDEN), c3), pl.BlockSpec((2, cw, B_CMP_HIDDEN), c3),
                  pl.BlockSpec((B_CMP_HIDDEN, HEAD_DIM), c2), pl.BlockSpec((HEAD_DIM, B_CMP_HIDDEN), c2)],
        out_specs=[pl.BlockSpec((1, 1, nch, HEAD_DIM), lambda b, g: (b, g, 0, 0)),
                   pl.BlockSpec((1, 1, HEAD_DIM, nch), lambda b, g: (b, g, 0, 0))],
        out_shape=[jax.ShapeDtypeStruct((bsz, B_GROUPS, nch, HEAD_DIM), BF16),
                   jax.ShapeDtypeStruct((bsz, B_GROUPS, HEAD_DIM, nch), BF16)],
        compiler_params=_params(2),
        name="compress",
    )(xk, xv, pe_k.reshape(2, cw), pe_v.reshape(2, cw),
      w1_k.astype(BF16).reshape(2, cw, B_CMP_HIDDEN), w1_v.astype(BF16).reshape(2, cw, B_CMP_HIDDEN),
      w2_k.astype(BF16), w2_v.T.astype(BF16))


def _nsa_kernel(qbt_ref, misct_ref, kcmp_ref, vcmpt_ref, ks_ref, vst_ref, kw_ref, vwt_ref, ovlt_ref,
                o_ref, chosen_ref, sa_ref, sb_ref, m_ref, acc_ref, out_ref, *, n_slc, top_n):
    qb = Q_BLOCK
    kc = KEY_CHUNK
    rep = B_REP
    slc_shift = int(np.log2(B_SLC_LEN))
    q0 = pl.program_id(1) * qb
    n_chunks = lax.div(q0 + qb + kc - 1, kc)
    tq = q0 + lax.broadcasted_iota(I32, (1, qb), 1)
    gates = misct_ref[0]
    n_cmp = kcmp_ref.shape[2]
    n_rows = chosen_ref.shape[1]
    cmp_end = lax.broadcasted_iota(I32, (n_cmp, qb), 0) * B_CMP_STRIDE + (B_CMP_LEN - 1)
    cap_c = _mask_cap(cmp_end <= tq)
    jj = lax.broadcasted_iota(I32, (n_rows, qb), 0)
    jj_f = jj.astype(F32)
    blk_q = lax.shift_right_logical(tq, slc_shift)
    forced = (jj == 0) | (jj == blk_q) | (jj == blk_q - 1)
    half = kc // 2
    row_h = lax.broadcasted_iota(I32, (half, qb), 0)
    w0 = pl.multiple_of(jnp.maximum(q0 - B_WINDOW, 0), qb)
    wlen = B_WINDOW + qb
    pos_w = w0 + lax.broadcasted_iota(I32, (wlen, qb), 0)
    cap_w = _mask_cap((pos_w <= tq) & (pos_w > tq - B_WINDOW))

    q4s = [_heads_on_lanes(qbt_ref[0, g * rep * HEAD_DIM:(g + 1) * rep * HEAD_DIM, :], rep, HEAD_DIM)
           for g in range(B_GROUPS)]
    vrows = [slice(g * HEAD_DIM, (g + 1) * HEAD_DIM) for g in range(B_GROUPS)]

    def add_branch(g, branch, o_t, first=False):
        for r in range(rep):
            h = g * rep + r
            rows = slice(h * HEAD_DIM, (h + 1) * HEAD_DIM)
            term = gates[branch * B_HEADS + h:branch * B_HEADS + h + 1, :] * o_t[:, r * qb:(r + 1) * qb]
            out_ref[rows, :] = term if first else out_ref[rows, :] + term

    for g in range(B_GROUPS):
        q4 = q4s[g]
        s_c = _dot(kcmp_ref[0, g], q4)
        p_c = [_masked_softmax_t(s_c[:, r * qb:(r + 1) * qb], cap_c) for r in range(rep)]
        add_branch(g, 0, _dot(vcmpt_ref[0, g], jnp.concatenate([p.astype(BF16) for p in p_c], axis=1)),
                   first=True)

        p_sum = p_c[0]
        for r in range(1, rep):
            p_sum = p_sum + p_c[r]
        hi = p_sum.astype(BF16)
        r1 = p_sum - hi.astype(F32)
        mid = r1.astype(BF16)
        lo = (r1 - mid.astype(F32)).astype(BF16)
        ovl = ovlt_ref[...]
        imp = _dot(ovl, hi) + _dot(ovl, mid) + _dot(ovl, lo)
        val = jnp.where(jj <= blk_q, imp + FORCE_BONUS * jnp.where(forced, 1.0, 0.0), NEG)
        val = jnp.where(jj < n_slc, val, -jnp.inf)

        def pick(i, carry):
            cur, chosen = carry
            mx = jnp.max(cur, axis=0, keepdims=True)
            first = jnp.min(jnp.where(cur == mx, jj_f, float(n_rows)), axis=0, keepdims=True)
            hit = jj_f == first
            return jnp.where(hit, -jnp.inf, cur), jnp.where(hit, 1.0, chosen)

        _, chosen = lax.fori_loop(0, top_n, pick, (val, jnp.zeros((n_rows, qb), F32)))
        chosen_ref[g] = chosen

        s_w = _dot(kw_ref[0, g, pl.ds(w0, wlen), :], q4)
        p_w = [_masked_softmax_t(s_w[:, r * qb:(r + 1) * qb], cap_w).astype(BF16) for r in range(rep)]
        add_branch(g, 2, _dot(vwt_ref[0, vrows[g], pl.ds(w0, wlen)], jnp.concatenate(p_w, axis=1)))

    m_ref[...] = jnp.full_like(m_ref, NEG)
    acc_ref[...] = jnp.zeros_like(acc_ref)

    def keys_at(i):
        return pl.ds(pl.multiple_of(i * half, half), half)

    def group_fns(g):
        def cap_fn(i):
            j0 = i * (half // B_SLC_LEN)
            blocks = [jnp.broadcast_to(chosen_ref[g, pl.ds(j0 + b, 1), :], (B_SLC_LEN, qb))
                      for b in range(half // B_SLC_LEN)]
            return _mask_cap((jnp.concatenate(blocks, axis=0) > 0.5) & (i * half + row_h <= tq))
        return (lambda i: _dot(ks_ref[0, g, keys_at(i), :], q4s[g]), cap_fn,
                lambda i: vst_ref[0, vrows[g], keys_at(i)])

    _pipelined_attention_t(n_chunks, [group_fns(g) for g in range(B_GROUPS)],
                           sa_ref, sb_ref, m_ref, acc_ref, rep)
    o_s = _finish_online_softmax_t(m_ref, acc_ref)
    for g in range(B_GROUPS):
        add_branch(g, 1, o_s[:, g * rep * qb:(g + 1) * rep * qb])

    o_ref[0] = out_ref[...].T.astype(o_ref.dtype)


def _nsa(qb_t, misc_t, kcmp, vcmp_t, ks, vs_t, kw, vw_t):
    bsz, _, s = qb_t.shape
    qblk = Q_BLOCK
    n_cmp = kcmp.shape[2]
    n_slc = s // B_SLC_LEN
    top_n = min(B_SLC_TOP, n_slc)
    n_rows = -(-n_slc // SUBLANES) * SUBLANES
    assert s % KEY_CHUNK == 0 and s >= B_WINDOW + qblk and KEY_CHUNK % B_SLC_LEN == 0
    slc_start = np.arange(n_rows)[:, None] * B_SLC_LEN
    cmp_start = np.arange(n_cmp)[None, :] * B_CMP_STRIDE
    overlap_t = ((cmp_start < slc_start + B_SLC_LEN) & (slc_start < cmp_start + B_CMP_LEN)
                 & (np.arange(n_rows)[:, None] < n_slc))
    overlap_t = jnp.asarray(overlap_t, BF16)
    feat = lambda rows: pl.BlockSpec((1, rows, qblk), lambda b, n: (b, 0, n))
    k_spec = pl.BlockSpec((1, B_GROUPS, s, HEAD_DIM), lambda b, n: (b, 0, 0, 0))
    vt_spec = pl.BlockSpec((1, B_KV_WIDTH, s), lambda b, n: (b, 0, 0))
    return pl.pallas_call(
        functools.partial(_nsa_kernel, n_slc=n_slc, top_n=top_n),
        grid=(bsz, s // qblk),
        in_specs=[feat(B_WIDTH), feat(_MISC_ROWS),
                  pl.BlockSpec((1, B_GROUPS, n_cmp, HEAD_DIM), lambda b, n: (b, 0, 0, 0)),
                  pl.BlockSpec((1, B_GROUPS, HEAD_DIM, n_cmp), lambda b, n: (b, 0, 0, 0)),
                  k_spec, vt_spec, k_spec, vt_spec,
                  pl.BlockSpec((n_rows, n_cmp), lambda b, n: (0, 0))],
        out_specs=pl.BlockSpec((1, qblk, B_WIDTH), lambda b, n: (b, n, 0)),
        out_shape=jax.ShapeDtypeStruct((bsz, s, B_WIDTH), BF16),
        scratch_shapes=[pltpu.VMEM((B_GROUPS, n_rows, qblk), F32),
                        pltpu.VMEM((KEY_CHUNK // 2, B_HEADS * qblk), F32),
                        pltpu.VMEM((KEY_CHUNK // 2, B_HEADS * qblk), F32),
                        pltpu.VMEM((1, B_HEADS * qblk), F32),
                        pltpu.VMEM((HEAD_DIM + ONES_ROWS, B_HEADS * qblk), F32),
                        pltpu.VMEM((B_WIDTH, qblk), F32)],
        compiler_params=_params(2),
        name="nsa",
    )(qb_t, misc_t, kcmp, vcmp_t, ks, vs_t, kw, vw_t, overlap_t)


def _merge_kernel(x_ref, mod_ref, ya_ref, yb_ref, wgm_ref, wa_ref, wb_ref, wo_ref, g_ref, b_ref,
                  o_ref, *, alpha):
    x = x_ref[0]
    d = x.shape[-1]
    sh = mod_ref[0, 3:4, :]
    sc = mod_ref[0, 4:5, :]
    gt = mod_ref[0, 5:6, :]
    u = (x * (1.0 + sc) + sh).astype(BF16)
    gm = jax.nn.sigmoid(_dot(u, wgm_ref[...]))
    merged = gm[:, :d] * _dot(ya_ref[0], wa_ref[...]) + gm[:, d:] * _dot(yb_ref[0], wb_ref[...])
    mix = _dot(merged.astype(BF16), wo_ref[...])
    o_ref[0] = _layer_norm(alpha * x + gt * mix, g_ref[...], b_ref[...])


def _merge(x, mod, ya, yb, w_gm, w_br_a, w_br_b, w_mix_out, ln_g, ln_b, alpha):
    bsz, s, d = x.shape
    tm = TOKEN_TILE
    c2 = lambda b, i: (0, 0)
    tok = lambda width: pl.BlockSpec((1, tm, width), lambda b, i: (b, i, 0))
    return pl.pallas_call(
        functools.partial(_merge_kernel, alpha=alpha),
        grid=(bsz, s // tm),
        in_specs=[tok(d),
                  pl.BlockSpec((1, N_MOD, d), lambda b, i: (b, 0, 0)),
                  tok(A_WIDTH), tok(B_WIDTH),
                  pl.BlockSpec((d, 2 * d), c2),
                  pl.BlockSpec((A_WIDTH, d), c2),
                  pl.BlockSpec((B_WIDTH, d), c2),
                  pl.BlockSpec((d, d), c2),
                  pl.BlockSpec((1, d), c2), pl.BlockSpec((1, d), c2)],
        out_specs=tok(d),
        out_shape=jax.ShapeDtypeStruct((bsz, s, d), F32),
        compiler_params=_params(2),
        name="merge",
    )(x, mod, ya, yb, w_gm.astype(BF16), w_br_a.astype(BF16), w_br_b.astype(BF16),
      w_mix_out.astype(BF16), ln_g.reshape(1, d), ln_b.reshape(1, d))


def kernel(x, c, w_ada, b_ada, ffn1_w_in, ffn1_w_out, ln1_g, ln1_b, w_mix_in, kv_norm_g, w_kv_up,
           cmp_pe_k, cmp_w1_k, cmp_w2_k, cmp_pe_v, cmp_w1_v, cmp_w2_v, w_br_a, w_br_b, w_mix_out,
           ln2_g, ln2_b, ffn2_w_in, ffn2_w_out, ln3_g, ln3_b):
    bsz, s, d = x.shape
    depth = w_ada.shape[0]
    alpha = (2.0 * depth) ** 0.25
    assert s % TOKEN_TILE == 0 and s % KEY_CHUNK == 0
    for l in range(depth):
        mod = _ada(c, w_ada[l], b_ada[l]).reshape(bsz, N_MOD, d)
        x = _ffn(x, mod, ffn1_w_in[l], ffn1_w_out[l], ln1_g[l], ln1_b[l], 0, alpha)
        qa_t, qb_t, qi_t, va_t, vs_t, vw_t, misc_t, ka, ki, ks, kw, cmp_in = _mixin(
            x, mod, w_mix_in[l], kv_norm_g[l], w_kv_up[l])
        ya = _dsa(qa_t, qi_t, misc_t, ki, ka, va_t)
        kcmp, vcmp_t = _compress(cmp_in, cmp_pe_k[l], cmp_w1_k[l], cmp_w2_k[l],
                                 cmp_pe_v[l], cmp_w1_v[l], cmp_w2_v[l])
        yb = _nsa(qb_t, misc_t, kcmp, vcmp_t, ks, vs_t, kw, vw_t)
        w_gm = w_mix_in[l][:, -2 * d:]
        x = _merge(x, mod, ya, yb, w_gm, w_br_a[l], w_br_b[l], w_mix_out[l], ln2_g[l], ln2_b[l], alpha)
        x = _ffn(x, mod, ffn2_w_in[l], ffn2_w_out[l], ln3_g[l], ln3_b[l], 6, alpha)
    return x
```
